```python
import jax, jax.numpy as jnp
from jax import lax
import numpy as np

D_MODEL = 4096
BATCH = 8
SEQ = 2048
DEPTH = 2

N_META = 16
RET_HEADS = 8
D_RET = D_MODEL // 2
RET_HEAD_DIM = D_RET // RET_HEADS
D_CONV = D_MODEL // 2
CONV_GROUPS = 8
CONV_WIDTH = 3
D_FF = 4 * D_MODEL
CHUNK = 128
ROPE_BASE = 10000.0
EPS = 1e-6
IN_SIZES = (D_RET, D_RET, D_RET, D_RET, D_CONV, D_CONV, D_CONV, D_MODEL, D_MODEL)
D_IN = sum(IN_SIZES)

kernel_name = "hybrid_retention_shortconv_gated_block"


def rmsnorm(x, g):
    xf = x.astype(jnp.float32)
    y = xf * lax.rsqrt(jnp.mean(jnp.square(xf), axis=-1, keepdims=True) + EPS)
    return (y * g.astype(jnp.float32)).astype(x.dtype)


def rotary(t, pos):
    d = t.shape[-1]
    inv_freq = ROPE_BASE ** (-jnp.arange(0, d, 2, dtype=jnp.float32) / d)
    ang = pos[:, None] * inv_freq[None, :]
    cos, sin = jnp.cos(ang), jnp.sin(ang)
    t1, t2 = t[..., : d // 2], t[..., d // 2:]
    return jnp.concatenate([t1 * cos - t2 * sin, t1 * sin + t2 * cos], axis=-1)


def chunkwise_retention(q, k, v):
    b, h, l, dk = q.shape
    dv = v.shape[-1]
    pad = (-l) % CHUNK
    padw = ((0, 0), (0, 0), (pad, 0), (0, 0))
    q, k, v = jnp.pad(q, padw), jnp.pad(k, padw), jnp.pad(v, padw)
    n = (l + pad) // CHUNK

    def to_chunks(t):
        return t.reshape(b, h, n, CHUNK, t.shape[-1]).transpose(2, 0, 1, 3, 4)

    qc, kc, vc = to_chunks(q), to_chunks(k), to_chunks(v)
    gamma = 1.0 - jnp.exp2(-5.0 - jnp.arange(h, dtype=jnp.float32))
    log_g = jnp.log(gamma)
    idx = jnp.arange(CHUNK, dtype=jnp.float32)
    rel = idx[:, None] - idx[None, :]
    inner_decay = jnp.where(rel >= 0, jnp.exp(log_g[:, None, None] * jnp.maximum(rel, 0.0)), 0.0)
    q_decay = jnp.exp(log_g[:, None] * (idx[None, :] + 1.0))
    k_decay = jnp.exp(log_g[:, None] * (CHUNK - 1.0 - idx[None, :]))
    chunk_decay = jnp.exp(log_g * CHUNK)

    def step(state, inp):
        qi, ki, vi = inp
        scores = jnp.einsum('bhqd,bhkd->bhqk', qi, ki) * inner_decay[None]
        inner = jnp.einsum('bhqk,bhkv->bhqv', scores, vi)
        cross = jnp.einsum('bhqd,bhdv->bhqv', qi, state) * q_decay[None, :, :, None]
        new_state = state * chunk_decay[None, :, None, None] + jnp.einsum(
            'bhkd,bhkv->bhdv', ki * k_decay[None, :, :, None], vi)
        return new_state, inner + cross

    state0 = jnp.zeros((b, h, dk, dv), jnp.float32)
    _, out = lax.scan(step, state0, (qc, kc, vc))
    out = out.transpose(1, 2, 0, 3, 4).reshape(b, h, n * CHUNK, dv)
    return out[:, :, pad:]


def hybrid_layer(x, pos, g_mix, w_in, conv_w, w_ret_out, w_conv_out, w_out, g_mlp, w_up, w_down):
    b, l, _ = x.shape
    u = rmsnorm(x, g_mix)
    split_points = [int(s) for s in np.cumsum(IN_SIZES)[:-1]]
    q, k, v, g_ret, c_gate, b_gate, z, gate_a, gate_b = jnp.split(u @ w_in, split_points, axis=-1)

    def heads(t):
        return t.reshape(b, l, RET_HEADS, RET_HEAD_DIM).transpose(0, 2, 1, 3).astype(jnp.float32)

    qh = rotary(heads(q), pos)
    kh = rotary(heads(k), pos) * (RET_HEAD_DIM ** -0.5)
    vh = heads(v)
    ret = chunkwise_retention(qh, kh, vh)
    ret = ret * lax.rsqrt(jnp.mean(jnp.square(ret), axis=-1, keepdims=True) + EPS)
    ret = ret.transpose(0, 2, 1, 3).reshape(b, l, D_RET).astype(x.dtype)
    y_a = (ret * jax.nn.silu(g_ret)) @ w_ret_out

    zc = c_gate * z
    zp = jnp.pad(zc, ((0, 0), (CONV_WIDTH - 1, 0), (0, 0)))
    conv = sum(conv_w[j] * zp[:, j:j + l, :] for j in range(CONV_WIDTH))
    y_b = (b_gate * conv) @ w_conv_out

    merged = jax.nn.sigmoid(gate_a) * y_a + jax.nn.sigmoid(gate_b) * y_b
    h = x + merged @ w_out

    u2 = rmsnorm(h, g_mlp)
    return h + jnp.square(jax.nn.relu(u2 @ w_up)) @ w_down


def setup_inputs(seed: int = 0) -> dict:
    key = jax.random.key(seed)
    ks = jax.random.split(key, 13)
    f32 = jnp.float32
    nrm = lambda k, shape, scale: jax.random.normal(k, shape, f32) * scale
    return {
        "x": nrm(ks[0], (BATCH, SEQ, D_MODEL), 1.0),
        "meta_tokens": nrm(ks[1], (N_META, D_MODEL), 1.0),
        "norm_mix_g": 1.0 + nrm(ks[2], (DEPTH, D_MODEL), 0.02),
        "w_in": nrm(ks[3], (DEPTH, D_MODEL, D_IN), D_MODEL ** -0.5),
        "conv_w": nrm(ks[4], (DEPTH, CONV_WIDTH, D_CONV), CONV_WIDTH ** -0.5),
        "w_ret_out": nrm(ks[5], (DEPTH, D_RET, D_MODEL), D_RET ** -0.5),
        "w_conv_out": nrm(ks[6], (DEPTH, D_CONV, D_MODEL), D_CONV ** -0.5),
        "w_out": nrm(ks[7], (DEPTH, D_MODEL, D_MODEL), D_MODEL ** -0.5),
        "norm_mlp_g": 1.0 + nrm(ks[8], (DEPTH, D_MODEL), 0.02),
        "w_up": nrm(ks[9], (DEPTH, D_MODEL, D_FF), D_MODEL ** -0.5),
        "w_down": nrm(ks[10], (DEPTH, D_FF, D_MODEL), D_FF ** -0.5),
        "final_norm_g": 1.0 + nrm(ks[11], (D_MODEL,), 0.02),
    }


def reference(x, meta_tokens, norm_mix_g, w_in, conv_w, w_ret_out, w_conv_out, w_out,
              norm_mlp_g, w_up, w_down, final_norm_g):
    b = x.shape[0]
    meta = jnp.broadcast_to(meta_tokens.astype(x.dtype)[None], (b, N_META, D_MODEL))
    h = jnp.concatenate([meta, x], axis=1)
    pos = jnp.arange(h.shape[1], dtype=jnp.float32)
    for i in range(DEPTH):
        h = hybrid_layer(h, pos, norm_mix_g[i], w_in[i], conv_w[i], w_ret_out[i], w_conv_out[i],
                         w_out[i], norm_mlp_g[i], w_up[i], w_down[i])
    h = rmsnorm(h, final_norm_g)
    return h[:, N_META:]
```

```python
import functools

import numpy as np
import jax
import jax.numpy as jnp
from jax import lax
from jax.experimental import pallas as pl
from jax.experimental.pallas import tpu as pltpu

N_META = 16
RET_HEADS = 8
CONV_WIDTH = 3
CHUNK = 128
ROPE_BASE = 10000.0
EPS = 1e-6

V7X_VMEM_BYTES = 64 * 1024 * 1024
SUBLANES = 8
MIB = 1024 * 1024

F32 = jnp.float32
BF16 = jnp.bfloat16

ROW_TILE_NORM = 256
ROW_TILE_RET = 1024
ROW_TILE_CONV = 1024
COL_TILE_CONV = 256
ROW_TILE_MERGE = 512
COL_TILE_MERGE = 512
ROW_TILE_RES = 512
K_TILE_RES = 512
ROW_TILE_UP = 1024
COL_TILE_UP = 1024


def _dot(a, b):
    return jnp.dot(a, b, preferred_element_type=F32)


def _params(n_axes, vmem_bytes):
    assert vmem_bytes <= V7X_VMEM_BYTES - 4 * MIB, vmem_bytes
    return pltpu.CompilerParams(
        dimension_semantics=("arbitrary",) * n_axes,
        vmem_limit_bytes=int(vmem_bytes),
    )


def _row_tile(m, want):
    return min(m, want)


def _rmsnorm_kernel(x_ref, g_ref, o_ref):
    x = x_ref[...]
    y = x * lax.rsqrt(jnp.mean(x * x, axis=-1, keepdims=True) + EPS)
    o_ref[...] = (y * g_ref[...]).astype(o_ref.dtype)


def _rmsnorm(x, g, out_dtype):
    m, d = x.shape
    tm = _row_tile(m, ROW_TILE_NORM)
    return pl.pallas_call(
        _rmsnorm_kernel,
        grid=(m // tm,),
        in_specs=[pl.BlockSpec((tm, d), lambda i: (i, 0)),
                  pl.BlockSpec((1, d), lambda i: (0, 0))],
        out_specs=pl.BlockSpec((tm, d), lambda i: (i, 0)),
        out_shape=jax.ShapeDtypeStruct((m, d), out_dtype),
        compiler_params=_params(1, 32 * MIB),
        name="rmsnorm",
    )(x, g.reshape(1, d))


def _retention_kernel(logg_ref, u_ref, w_ref, cos_ref, sin_ref, s0_ref, *rest,
                      tm, hd, emit_state):
    if emit_state:
        a_ref, sout_ref, state_ref, q_s, k_s, v_s, g_s = rest
    else:
        a_ref, state_ref, q_s, k_s, v_s, g_s = rest
    rb = pl.program_id(1)
    h = pl.program_id(2)
    half = hd // 2

    @pl.when(rb == 0)
    def _():
        state_ref[h] = s0_ref[...]

    proj = _dot(u_ref[...], w_ref[...])
    cos = cos_ref[...]
    sin = sin_ref[...]

    def rot(t):
        t1, t2 = t[:, :half], t[:, half:]
        return jnp.concatenate([t1 * cos - t2 * sin, t1 * sin + t2 * cos], axis=-1)

    q_s[...] = rot(proj[:, 0 * hd:1 * hd]).astype(BF16)
    k_s[...] = rot(proj[:, 1 * hd:2 * hd]) * (hd ** -0.5)
    v_s[...] = proj[:, 2 * hd:3 * hd].astype(BF16)
    g = proj[:, 3 * hd:4 * hd]
    g_s[...] = g * jax.nn.sigmoid(g)

    lg = logg_ref[h]
    ri = lax.broadcasted_iota(jnp.int32, (CHUNK, CHUNK), 0)
    ci = lax.broadcasted_iota(jnp.int32, (CHUNK, CHUNK), 1)
    rel = (ri - ci).astype(F32)
    inner_decay = jnp.where(rel >= 0, jnp.exp(lg * jnp.maximum(rel, 0.0)), 0.0)
    jj = lax.broadcasted_iota(jnp.int32, (CHUNK, 1), 0).astype(F32)
    q_decay = jnp.exp(lg * (jj + 1.0))
    k_decay = jnp.exp(lg * (CHUNK - 1.0 - jj))
    chunk_decay = jnp.exp(jnp.full((1, hd), lg * CHUNK, F32))

    def chunk(c, carry):
        rows = pl.ds(pl.multiple_of(c * CHUNK, CHUNK), CHUNK)
        qc = q_s[rows, :]
        kc = k_s[rows, :]
        vc = v_s[rows, :]
        st = state_ref[h]
        scores = lax.dot_general(qc, kc.astype(BF16), (((1,), (1,)), ((), ())),
                                 preferred_element_type=F32) * inner_decay
        inner = _dot(scores.astype(BF16), vc)
        cross = _dot(qc, st.astype(BF16)) * q_decay
        state_ref[h] = st * chunk_decay + lax.dot_general(
            (kc * k_decay).astype(BF16), vc, (((0,), (0,)), ((), ())),
            preferred_element_type=F32)
        o = inner + cross
        o = o * lax.rsqrt(jnp.mean(o * o, axis=-1, keepdims=True) + EPS)
        a_ref[rows, :] = (o * g_s[rows, :]).astype(a_ref.dtype)
        return carry

    lax.fori_loop(0, tm // CHUNK, chunk, 0, unroll=True)

    if emit_state:
        sout_ref[...] = state_ref[h]


def _retention(u, w_ret, cos, sin, state0, log_gamma, *, batch, emit_state):
    m, d = u.shape
    lb = m // batch
    heads = RET_HEADS
    hd = w_ret.shape[1] // (4 * heads)
    tm = _row_tile(lb, ROW_TILE_RET)
    nrb = lb // tm
    out_shape = [jax.ShapeDtypeStruct((m, heads * hd), BF16)]
    out_specs = [pl.BlockSpec((tm, hd), lambda b, r, h: (b * nrb + r, h))]
    if emit_state:
        out_shape.append(jax.ShapeDtypeStruct((heads, hd, hd), F32))
        out_specs.append(pl.BlockSpec((None, hd, hd), lambda b, r, h: (h, 0, 0)))
    vmem = (2 * tm * d * 2 + 2 * d * 4 * hd * 2 + 3 * tm * 4 * hd * 4
            + heads * hd * hd * 4 + 6 * MIB)
    res = pl.pallas_call(
        functools.partial(_retention_kernel, tm=tm, hd=hd, emit_state=emit_state),
        grid=(batch, nrb, heads),
        in_specs=[
            pl.BlockSpec(memory_space=pltpu.SMEM),
            pl.BlockSpec((tm, d), lambda b, r, h: (b * nrb + r, 0)),
            pl.BlockSpec((d, 4 * hd), lambda b, r, h: (0, h)),
            pl.BlockSpec((tm, hd // 2), lambda b, r, h: (r, 0)),
            pl.BlockSpec((tm, hd // 2), lambda b, r, h: (r, 0)),
            pl.BlockSpec((None, hd, hd), lambda b, r, h: (h, 0, 0)),
        ],
        out_specs=out_specs,
        out_shape=out_shape,
        scratch_shapes=[
            pltpu.VMEM((heads, hd, hd), F32),
            pltpu.VMEM((tm, hd), BF16),
            pltpu.VMEM((tm, hd), F32),
            pltpu.VMEM((tm, hd), BF16),
            pltpu.VMEM((tm, hd), F32),
        ],
        compiler_params=_params(3, vmem),
        name="retention",
    )(log_gamma, u, w_ret, cos, sin, state0)
    return res if emit_state else res[0]


def _conv_kernel(u_ref, w_ref, cw_ref, halo_ref, *rest, tm, tn, nrb, emit_tail):
    if emit_tail:
        b1_ref, tail_ref, carry_ref = rest
    else:
        b1_ref, carry_ref = rest
    i = pl.program_id(0)
    j = pl.program_id(1)

    @pl.when(i % nrb == 0)
    def _():
        carry_ref[j] = halo_ref[...]

    proj = _dot(u_ref[...], w_ref[...])
    zc = proj[:, :tn] * proj[:, 2 * tn:]
    prev = carry_ref[j]
    p1 = prev[SUBLANES - 1:SUBLANES, :]
    p2 = prev[SUBLANES - 2:SUBLANES - 1, :]
    row = lax.broadcasted_iota(jnp.int32, zc.shape, 0)
    z1 = jnp.where(row == 0, p1, pltpu.roll(zc, 1, 0))
    z2 = jnp.where(row == 0, p2, jnp.where(row == 1, p1, pltpu.roll(zc, 2, 0)))
    cw = cw_ref[...]
    conv = cw[0:1, :] * z2 + cw[1:2, :] * z1 + cw[2:3, :] * zc
    b1_ref[...] = (proj[:, tn:2 * tn] * conv).astype(b1_ref.dtype)
    tail = zc[tm - SUBLANES:, :]
    carry_ref[j] = tail
    if emit_tail:
        tail_ref[...] = tail


def _conv_tile(dc):
    return min(dc, COL_TILE_CONV)


def _conv_mixer(u, w_cbz, conv_w, halo, *, batch, emit_tail):
    m, d = u.shape
    lb = m // batch
    dc = conv_w.shape[1]
    tn = _conv_tile(dc)
    tm = _row_tile(lb, ROW_TILE_CONV)
    nrb = lb // tm
    ni, nj = m // tm, dc // tn
    out_shape = [jax.ShapeDtypeStruct((m, dc), BF16)]
    out_specs = [pl.BlockSpec((tm, tn), lambda i, j: (i, j))]
    if emit_tail:
        out_shape.append(jax.ShapeDtypeStruct((ni * SUBLANES, dc), F32))
        out_specs.append(pl.BlockSpec((SUBLANES, tn), lambda i, j: (i, j)))
    vmem = 2 * tm * d * 2 + 2 * d * 3 * tn * 2 + 3 * tm * 3 * tn * 4 + 4 * MIB
    res = pl.pallas_call(
        functools.partial(_conv_kernel, tm=tm, tn=tn, nrb=nrb, emit_tail=emit_tail),
        grid=(ni, nj),
        in_specs=[
            pl.BlockSpec((tm, d), lambda i, j: (i, 0)),
            pl.BlockSpec((d, 3 * tn), lambda i, j: (0, j)),
            pl.BlockSpec((CONV_WIDTH, tn), lambda i, j: (0, j)),
            pl.BlockSpec((SUBLANES, tn), lambda i, j: (0, j)),
        ],
        out_specs=out_specs,
        out_shape=out_shape,
        scratch_shapes=[pltpu.VMEM((nj, SUBLANES, tn), F32)],
        compiler_params=_params(2, vmem),
        name="conv_mixer",
    )(u, w_cbz, conv_w, halo)
    return res if emit_tail else res[0]


def _merge_kernel(a_ref, b_ref, u_ref, wro_ref, wco_ref, wg_ref, o_ref, *, tn):
    ya = _dot(a_ref[...], wro_ref[...])
    yb = _dot(b_ref[...], wco_ref[...])
    gates = _dot(u_ref[...], wg_ref[...])
    o_ref[...] = (jax.nn.sigmoid(gates[:, :tn]) * ya
                  + jax.nn.sigmoid(gates[:, tn:]) * yb).astype(o_ref.dtype)


def _merge_tile(d):
    return min(d, COL_TILE_MERGE)


def _merge(a1, b1, u, w_ro, w_co, w_gate):
    m, d = u.shape
    dr = a1.shape[1]
    dc = b1.shape[1]
    tn = _merge_tile(d)
    tm = _row_tile(m, ROW_TILE_MERGE)
    vmem = (2 * tm * (dr + dc + d) * 2 + 2 * (dr + dc + 2 * d) * tn * 2
            + 8 * tm * tn * 4 + 8 * MIB)
    return pl.pallas_call(
        functools.partial(_merge_kernel, tn=tn),
        grid=(m // tm, d // tn),
        in_specs=[
            pl.BlockSpec((tm, dr), lambda i, j: (i, 0)),
            pl.BlockSpec((tm, dc), lambda i, j: (i, 0)),
            pl.BlockSpec((tm, d), lambda i, j: (i, 0)),
            pl.BlockSpec((dr, tn), lambda i, j: (0, j)),
            pl.BlockSpec((dc, tn), lambda i, j: (0, j)),
            pl.BlockSpec((d, 2 * tn), lambda i, j: (0, j)),
        ],
        out_specs=pl.BlockSpec((tm, tn), lambda i, j: (i, j)),
        out_shape=jax.ShapeDtypeStruct((m, d), BF16),
        compiler_params=_params(2, vmem),
        name="merge",
    )(a1, b1, u, w_ro, w_co, w_gate)


def _residual_kernel(a_ref, w_ref, h_ref, g_ref, *rest, nk, emit_h):
    if emit_h:
        acc_ref, u_ref = rest
    else:
        u_ref, acc_ref = rest
    k = pl.program_id(1)

    @pl.when(k == 0)
    def _():
        acc_ref[...] = h_ref[...]

    acc_ref[...] += _dot(a_ref[...], w_ref[...])

    @pl.when(k == nk - 1)
    def _():
        hn = acc_ref[...]
        y = hn * lax.rsqrt(jnp.mean(hn * hn, axis=-1, keepdims=True) + EPS)
        u_ref[...] = (y * g_ref[...]).astype(u_ref.dtype)


def _residual_norm(a, w, h, g, *, emit_h, u_dtype):
    m, kdim = a.shape
    d = w.shape[1]
    tm = _row_tile(m, ROW_TILE_RES)
    tk = min(kdim, K_TILE_RES)
    nk = kdim // tk
    u_bytes = jnp.dtype(u_dtype).itemsize
    out_shape = [jax.ShapeDtypeStruct((m, d), u_dtype)]
    out_specs = [pl.BlockSpec((tm, d), lambda i, k: (i, 0))]
    scratch = []
    if emit_h:
        out_shape.insert(0, jax.ShapeDtypeStruct((m, d), F32))
        out_specs.insert(0, pl.BlockSpec((tm, d), lambda i, k: (i, 0)))
    else:
        scratch.append(pltpu.VMEM((tm, d), F32))
    acc_bytes = (2 if emit_h else 1) * tm * d * 4
    vmem = (2 * tm * tk * 2 + 2 * tk * d * 2 + tm * d * 4 + acc_bytes
            + 2 * tm * d * u_bytes + tm * d * 4 + 4 * MIB)
    res = pl.pallas_call(
        functools.partial(_residual_kernel, nk=nk, emit_h=emit_h),
        grid=(m // tm, nk),
        in_specs=[
            pl.BlockSpec((tm, tk), lambda i, k: (i, k)),
            pl.BlockSpec((tk, d), lambda i, k: (k, 0)),
            pl.BlockSpec((tm, d), lambda i, k: (i, 0), pipeline_mode=pl.Buffered(1)),
            pl.BlockSpec((1, d), lambda i, k: (0, 0)),
        ],
        out_specs=out_specs,
        out_shape=out_shape,
        scratch_shapes=scratch,
        compiler_params=_params(2, vmem),
        name="residual_norm",
    )(a, w, h, g.reshape(1, d))
    return res if emit_h else res[0]


def _up_kernel(u_ref, w_ref, o_ref):
    t = jnp.maximum(_dot(u_ref[...], w_ref[...]), 0.0)
    o_ref[...] = (t * t).astype(o_ref.dtype)


def _up(u, w_up):
    m, d = u.shape
    f = w_up.shape[1]
    tm = _row_tile(m, ROW_TILE_UP)
    tn = min(f, COL_TILE_UP)
    vmem = 2 * tm * d * 2 + 2 * d * tn * 2 + 2 * tm * tn * 2 + 4 * tm * tn * 4 + 4 * MIB
    return pl.pallas_call(
        _up_kernel,
        grid=(m // tm, f // tn),
        in_specs=[pl.BlockSpec((tm, d), lambda i, j: (i, 0)),
                  pl.BlockSpec((d, tn), lambda i, j: (0, j))],
        out_specs=pl.BlockSpec((tm, tn), lambda i, j: (i, j)),
        out_shape=jax.ShapeDtypeStruct((m, f), BF16),
        compiler_params=_params(2, vmem),
        name="mlp_up",
    )(u, w_up)


def _layer_weights(w_in, w_ret_out, w_conv_out, w_out, w_up, w_down, d):
    dr = d // 2
    dc = d // 2
    hd = dr // RET_HEADS
    tn_c = _conv_tile(dc)
    tn_m = _merge_tile(d)
    w_in = w_in.astype(BF16)
    w_ret = w_in[:, :4 * dr].reshape(d, 4, RET_HEADS, hd).transpose(0, 2, 1, 3).reshape(d, 4 * dr)
    w_cbz = w_in[:, 4 * dr:4 * dr + 3 * dc].reshape(d, 3, dc // tn_c, tn_c)
    w_cbz = w_cbz.transpose(0, 2, 1, 3).reshape(d, 3 * dc)
    w_gate = w_in[:, 4 * dr + 3 * dc:].reshape(d, 2, d // tn_m, tn_m)
    w_gate = w_gate.transpose(0, 2, 1, 3).reshape(d, 2 * d)
    return dict(w_ret=w_ret, w_cbz=w_cbz, w_gate=w_gate,
                w_ro=w_ret_out.astype(BF16), w_co=w_conv_out.astype(BF16),
                w_o=w_out.astype(BF16), w_up=w_up.astype(BF16), w_down=w_down.astype(BF16))


def _rope_tables(pos, hd):
    inv_freq = ROPE_BASE ** (-jnp.arange(0, hd, 2, dtype=F32) / hd)
    ang = pos[:, None] * inv_freq[None, :]
    return jnp.cos(ang), jnp.sin(ang)


def kernel(x, meta_tokens, norm_mix_g, w_in, conv_w, w_ret_out, w_conv_out, w_out,
           norm_mlp_g, w_up, w_down, final_norm_g):
    batch, seq, d = x.shape
    depth = w_in.shape[0]
    n_meta = meta_tokens.shape[0]
    dr = d // 2
    dc = d // 2
    hd = dr // RET_HEADS
    assert seq % CHUNK == 0 and n_meta <= CHUNK
    meta_pad = CHUNK - n_meta

    layers = [_layer_weights(w_in[i], w_ret_out[i], w_conv_out[i], w_out[i], w_up[i],
                             w_down[i], d) for i in range(depth)]
    log_gamma = jnp.asarray(
        np.log1p(-np.exp2(-5.0 - np.arange(RET_HEADS, dtype=np.float64))), F32)
    cos_x, sin_x = _rope_tables(jnp.arange(seq, dtype=F32) + n_meta, hd)
    cos_m, sin_m = _rope_tables(jnp.arange(CHUNK, dtype=F32) - meta_pad, hd)

    hm = jnp.concatenate([jnp.zeros((meta_pad, d), F32), meta_tokens.astype(F32)], axis=0)
    um = _rmsnorm(hm, norm_mix_g[0], BF16)
    zero_state = jnp.zeros((RET_HEADS, hd, hd), F32)
    zero_halo = jnp.zeros((SUBLANES, dc), F32)
    states, halos = [], []
    for i, lw in enumerate(layers):
        a1, state = _retention(um, lw["w_ret"], cos_m, sin_m, zero_state, log_gamma,
                               batch=1, emit_state=True)
        b1, tail = _conv_mixer(um, lw["w_cbz"], conv_w[i], zero_halo, batch=1, emit_tail=True)
        states.append(state)
        halos.append(tail)
        if i + 1 < depth:
            merged = _merge(a1, b1, um, lw["w_ro"], lw["w_co"], lw["w_gate"])
            hm, u2 = _residual_norm(merged, lw["w_o"], hm, norm_mlp_g[i], emit_h=True, u_dtype=BF16)
            hid = _up(u2, lw["w_up"])
            hm, um = _residual_norm(hid, lw["w_down"], hm, norm_mix_g[i + 1], emit_h=True,
                                    u_dtype=BF16)

    h = x.reshape(batch * seq, d)
    u = _rmsnorm(h, norm_mix_g[0], BF16)
    out = None
    for i, lw in enumerate(layers):
        a1 = _retention(u, lw["w_ret"], cos_x, sin_x, states[i], log_gamma,
                        batch=batch, emit_state=False)
        b1 = _conv_mixer(u, lw["w_cbz"], conv_w[i], halos[i], batch=batch, emit_tail=False)
        merged = _merge(a1, b1, u, lw["w_ro"], lw["w_co"], lw["w_gate"])
        h, u2 = _residual_norm(merged, lw["w_o"], h, norm_mlp_g[i], emit_h=True, u_dtype=BF16)
        hid = _up(u2, lw["w_up"])
        if i + 1 < depth:
            h, u = _residual_norm(hid, lw["w_down"], h, norm_mix_g[i + 1], emit_h=True,
                                  u_dtype=BF16)
        else:
            out = _residual_norm(hid, lw["w_down"], h, final_norm_g, emit_h=False, u_dtype=F32)
    return out.reshape(batch, seq, d)
```

```python
import functools

import numpy as np
import jax
import jax.numpy as jnp
from jax import lax
from jax.experimental import pallas as pl
from jax.experimental.pallas import tpu as pltpu

N_META = 16
RET_HEADS = 8
CONV_WIDTH = 3
CHUNK = 128
ROPE_BASE = 10000.0
EPS = 1e-6

V7X_VMEM_BYTES = 64 * 1024 * 1024
SUBLANES = 8
MIB = 1024 * 1024

F32 = jnp.float32
BF16 = jnp.bfloat16

ROW_TILE_NORM = 256
ROW_TILE_RET = 1024
ROW_TILE_CONV = 1024
COL_TILE_CONV = 256
ROW_TILE_MERGE = 512
COL_TILE_MERGE = 512
ROW_TILE_RES = 1024
COL_TILE_RES_O = 512
COL_TILE_RES_DOWN = 256
K_SPLITS_DOWN = 2
ROW_TILE_UP = 1024
COL_TILE_UP = 1024
CAST_BLOCK_ELEMS = 1024 * 1024


def _dot(a, b):
    return jnp.dot(a, b, preferred_element_type=F32)


def _params(n_axes, vmem_bytes):
    assert vmem_bytes <= V7X_VMEM_BYTES - 4 * MIB, vmem_bytes
    return pltpu.CompilerParams(
        dimension_semantics=("arbitrary",) * n_axes,
        vmem_limit_bytes=int(vmem_bytes),
    )


def _row_tile(m, want):
    return min(m, want)


def _rmsnorm_kernel(x_ref, g_ref, o_ref):
    x = x_ref[...]
    y = x * lax.rsqrt(jnp.mean(x * x, axis=-1, keepdims=True) + EPS)
    o_ref[...] = (y * g_ref[...]).astype(o_ref.dtype)


def _rmsnorm(x, g, out_dtype):
    m, d = x.shape
    tm = _row_tile(m, ROW_TILE_NORM)
    return pl.pallas_call(
        _rmsnorm_kernel,
        grid=(m // tm,),
        in_specs=[pl.BlockSpec((tm, d), lambda i: (i, 0)),
                  pl.BlockSpec((1, d), lambda i: (0, 0))],
        out_specs=pl.BlockSpec((tm, d), lambda i: (i, 0)),
        out_shape=jax.ShapeDtypeStruct((m, d), out_dtype),
        compiler_params=_params(1, 32 * MIB),
        name="rmsnorm",
    )(x, g.reshape(1, d))


def _retention_kernel(logg_ref, u_ref, w_ref, cos_ref, sin_ref, s0_ref, *rest,
                      tm, hd, emit_state):
    if emit_state:
        a_ref, sout_ref, state_ref, q_s, k_s, v_s, g_s = rest
    else:
        a_ref, state_ref, q_s, k_s, v_s, g_s = rest
    rb = pl.program_id(1)
    h = pl.program_id(2)
    half = hd // 2

    @pl.when(rb == 0)
    def _():
        state_ref[h] = s0_ref[...]

    proj = _dot(u_ref[...], w_ref[...])
    cos = cos_ref[...]
    sin = sin_ref[...]

    def rot(t):
        t1, t2 = t[:, :half], t[:, half:]
        return jnp.concatenate([t1 * cos - t2 * sin, t1 * sin + t2 * cos], axis=-1)

    q_s[...] = rot(proj[:, 0 * hd:1 * hd]).astype(BF16)
    k_s[...] = rot(proj[:, 1 * hd:2 * hd]) * (hd ** -0.5)
    v_s[...] = proj[:, 2 * hd:3 * hd].astype(BF16)
    g = proj[:, 3 * hd:4 * hd]
    g_s[...] = g * jax.nn.sigmoid(g)

    lg = logg_ref[h]
    ri = lax.broadcasted_iota(jnp.int32, (CHUNK, CHUNK), 0)
    ci = lax.broadcasted_iota(jnp.int32, (CHUNK, CHUNK), 1)
    rel = (ri - ci).astype(F32)
    inner_decay = jnp.where(rel >= 0, jnp.exp(lg * jnp.maximum(rel, 0.0)), 0.0)
    jj = lax.broadcasted_iota(jnp.int32, (CHUNK, 1), 0).astype(F32)
    q_decay = jnp.exp(lg * (jj + 1.0))
    k_decay = jnp.exp(lg * (CHUNK - 1.0 - jj))
    chunk_decay = jnp.exp(jnp.full((1, hd), lg * CHUNK, F32))

    def chunk(c, carry):
        rows = pl.ds(pl.multiple_of(c * CHUNK, CHUNK), CHUNK)
        qc = q_s[rows, :]
        kc = k_s[rows, :]
        vc = v_s[rows, :]
        st = state_ref[h]
        scores = lax.dot_general(qc, kc.astype(BF16), (((1,), (1,)), ((), ())),
                                 preferred_element_type=F32) * inner_decay
        inner = _dot(scores.astype(BF16), vc)
        cross = _dot(qc, st.astype(BF16)) * q_decay
        state_ref[h] = st * chunk_decay + lax.dot_general(
            (kc * k_decay).astype(BF16), vc, (((0,), (0,)), ((), ())),
            preferred_element_type=F32)
        o = inner + cross
        o = o * lax.rsqrt(jnp.mean(o * o, axis=-1, keepdims=True) + EPS)
        a_ref[rows, :] = (o * g_s[rows, :]).astype(a_ref.dtype)
        return carry

    lax.fori_loop(0, tm // CHUNK, chunk, 0, unroll=True)

    if emit_state:
        sout_ref[...] = state_ref[h]


def _retention(u, w_ret, cos, sin, state0, log_gamma, *, batch, emit_state):
    m, d = u.shape
    lb = m // batch
    heads = RET_HEADS
    hd = w_ret.shape[1] // (4 * heads)
    tm = _row_tile(lb, ROW_TILE_RET)
    nrb = lb // tm
    out_shape = [jax.ShapeDtypeStruct((m, heads * hd), BF16)]
    out_specs = [pl.BlockSpec((tm, hd), lambda b, r, h: (b * nrb + r, h))]
    if emit_state:
        out_shape.append(jax.ShapeDtypeStruct((heads, hd, hd), F32))
        out_specs.append(pl.BlockSpec((None, hd, hd), lambda b, r, h: (h, 0, 0)))
    vmem = (2 * tm * d * 2 + 2 * d * 4 * hd * 2 + 3 * tm * 4 * hd * 4
            + heads * hd * hd * 4 + 6 * MIB)
    res = pl.pallas_call(
        functools.partial(_retention_kernel, tm=tm, hd=hd, emit_state=emit_state),
        grid=(batch, nrb, heads),
        in_specs=[
            pl.BlockSpec(memory_space=pltpu.SMEM),
            pl.BlockSpec((tm, d), lambda b, r, h: (b * nrb + r, 0)),
            pl.BlockSpec((d, 4 * hd), lambda b, r, h: (0, h)),
            pl.BlockSpec((tm, hd // 2), lambda b, r, h: (r, 0)),
            pl.BlockSpec((tm, hd // 2), lambda b, r, h: (r, 0)),
            pl.BlockSpec((None, hd, hd), lambda b, r, h: (h, 0, 0)),
        ],
        out_specs=out_specs,
        out_shape=out_shape,
        scratch_shapes=[
            pltpu.VMEM((heads, hd, hd), F32),
            pltpu.VMEM((tm, hd), BF16),
            pltpu.VMEM((tm, hd), F32),
            pltpu.VMEM((tm, hd), BF16),
            pltpu.VMEM((tm, hd), F32),
        ],
        compiler_params=_params(3, vmem),
        name="retention",
    )(log_gamma, u, w_ret, cos, sin, state0)
    return res if emit_state else res[0]


def _conv_kernel(u_ref, w_ref, cw_ref, halo_ref, *rest, tm, tn, nrb, emit_tail):
    if emit_tail:
        b1_ref, tail_ref, carry_ref = rest
    else:
        b1_ref, carry_ref = rest
    i = pl.program_id(0)
    j = pl.program_id(1)

    @pl.when(i % nrb == 0)
    def _():
        carry_ref[j] = halo_ref[...]

    proj = _dot(u_ref[...], w_ref[...])
    zc = proj[:, :tn] * proj[:, 2 * tn:]
    prev = carry_ref[j]
    p1 = prev[SUBLANES - 1:SUBLANES, :]
    p2 = prev[SUBLANES - 2:SUBLANES - 1, :]
    row = lax.broadcasted_iota(jnp.int32, zc.shape, 0)
    z1 = jnp.where(row == 0, p1, pltpu.roll(zc, 1, 0))
    z2 = jnp.where(row == 0, p2, jnp.where(row == 1, p1, pltpu.roll(zc, 2, 0)))
    cw = cw_ref[...]
    conv = cw[0:1, :] * z2 + cw[1:2, :] * z1 + cw[2:3, :] * zc
    b1_ref[...] = (proj[:, tn:2 * tn] * conv).astype(b1_ref.dtype)
    tail = zc[tm - SUBLANES:, :]
    carry_ref[j] = tail
    if emit_tail:
        tail_ref[...] = tail


def _conv_tile(dc):
    return min(dc, COL_TILE_CONV)


def _conv_mixer(u, w_cbz, conv_w, halo, *, batch, emit_tail):
    m, d = u.shape
    lb = m // batch
    dc = conv_w.shape[1]
    tn = _conv_tile(dc)
    tm = _row_tile(lb, ROW_TILE_CONV)
    nrb = lb // tm
    ni, nj = m // tm, dc // tn
    out_shape = [jax.ShapeDtypeStruct((m, dc), BF16)]
    out_specs = [pl.BlockSpec((tm, tn), lambda i, j: (i, j))]
    if emit_tail:
        out_shape.append(jax.ShapeDtypeStruct((ni * SUBLANES, dc), F32))
        out_specs.append(pl.BlockSpec((SUBLANES, tn), lambda i, j: (i, j)))
    vmem = 2 * tm * d * 2 + 2 * d * 3 * tn * 2 + 3 * tm * 3 * tn * 4 + 4 * MIB
    res = pl.pallas_call(
        functools.partial(_conv_kernel, tm=tm, tn=tn, nrb=nrb, emit_tail=emit_tail),
        grid=(ni, nj),
        in_specs=[
            pl.BlockSpec((tm, d), lambda i, j: (i, 0)),
            pl.BlockSpec((d, 3 * tn), lambda i, j: (0, j)),
            pl.BlockSpec((CONV_WIDTH, tn), lambda i, j: (0, j)),
            pl.BlockSpec((SUBLANES, tn), lambda i, j: (0, j)),
        ],
        out_specs=out_specs,
        out_shape=out_shape,
        scratch_shapes=[pltpu.VMEM((nj, SUBLANES, tn), F32)],
        compiler_params=_params(2, vmem),
        name="conv_mixer",
    )(u, w_cbz, conv_w, halo)
    return res if emit_tail else res[0]


def _merge_kernel(a_ref, b_ref, u_ref, wro_ref, wco_ref, wg_ref, o_ref, *, tn):
    ya = _dot(a_ref[...], wro_ref[...])
    yb = _dot(b_ref[...], wco_ref[...])
    gates = _dot(u_ref[...], wg_ref[...])
    o_ref[...] = (jax.nn.sigmoid(gates[:, :tn]) * ya
                  + jax.nn.sigmoid(gates[:, tn:]) * yb).astype(o_ref.dtype)


def _merge_tile(d):
    return min(d, COL_TILE_MERGE)


def _merge(a1, b1, u, w_ro, w_co, w_gate):
    m, d = u.shape
    dr = a1.shape[1]
    dc = b1.shape[1]
    tn = _merge_tile(d)
    tm = _row_tile(m, ROW_TILE_MERGE)
    vmem = (2 * tm * (dr + dc + d) * 2 + 2 * (dr + dc + 2 * d) * tn * 2
            + 8 * tm * tn * 4 + 8 * MIB)
    return pl.pallas_call(
        functools.partial(_merge_kernel, tn=tn),
        grid=(m // tm, d // tn),
        in_specs=[
            pl.BlockSpec((tm, dr), lambda i, j: (i, 0)),
            pl.BlockSpec((tm, dc), lambda i, j: (i, 0)),
            pl.BlockSpec((tm, d), lambda i, j: (i, 0)),
            pl.BlockSpec((dr, tn), lambda i, j: (0, j)),
            pl.BlockSpec((dc, tn), lambda i, j: (0, j)),
            pl.BlockSpec((d, 2 * tn), lambda i, j: (0, j)),
        ],
        out_specs=pl.BlockSpec((tm, tn), lambda i, j: (i, j)),
        out_shape=jax.ShapeDtypeStruct((m, d), BF16),
        compiler_params=_params(2, vmem),
        name="merge",
    )(a1, b1, u, w_ro, w_co, w_gate)


def _residual_kernel(a_ref, w_ref, h_ref, o_ref):
    o_ref[...] = h_ref[...] + _dot(a_ref[...], w_ref[...])


def _residual(a, w_tiles, h, *, k_splits):
    m, kdim = a.shape
    nt, _, tn = w_tiles.shape
    kb = kdim // k_splits
    tm = _row_tile(m, ROW_TILE_RES)
    vmem = 2 * tm * kb * 2 + 2 * kb * tn * 2 + 5 * tm * tn * 4 + 4 * MIB
    for s in range(k_splits):
        h = pl.pallas_call(
            _residual_kernel,
            grid=(m // tm, nt),
            in_specs=[
                pl.BlockSpec((tm, kb), lambda i, j, s=s: (i, s)),
                pl.BlockSpec((None, kb, tn), lambda i, j, s=s: (j, s, 0)),
                pl.BlockSpec((tm, tn), lambda i, j: (i, j)),
            ],
            out_specs=pl.BlockSpec((tm, tn), lambda i, j: (i, j)),
            out_shape=jax.ShapeDtypeStruct((m, nt * tn), F32),
            compiler_params=_params(2, vmem),
            name="residual",
        )(a, w_tiles, h)
    return h


def _up_kernel(u_ref, w_ref, o_ref):
    t = jnp.maximum(_dot(u_ref[...], w_ref[...]), 0.0)
    o_ref[...] = (t * t).astype(o_ref.dtype)


def _up(u, w_up):
    m, d = u.shape
    f = w_up.shape[1]
    tm = _row_tile(m, ROW_TILE_UP)
    tn = min(f, COL_TILE_UP)
    vmem = 2 * tm * d * 2 + 2 * d * tn * 2 + 2 * tm * tn * 2 + 4 * tm * tn * 4 + 4 * MIB
    return pl.pallas_call(
        _up_kernel,
        grid=(m // tm, f // tn),
        in_specs=[pl.BlockSpec((tm, d), lambda i, j: (i, 0)),
                  pl.BlockSpec((d, tn), lambda i, j: (0, j))],
        out_specs=pl.BlockSpec((tm, tn), lambda i, j: (i, j)),
        out_shape=jax.ShapeDtypeStruct((m, f), BF16),
        compiler_params=_params(2, vmem),
        name="mlp_up",
    )(u, w_up)


def _cast_kernel(x_ref, o_ref):
    o_ref[...] = x_ref[...].astype(o_ref.dtype)


def _cast_regroup(w, layer, *, col0, sections, width, tile, tile_major=False):
    kdim = w.shape[1]
    tiles_per_section = width // tile
    n_tiles = sections * tiles_per_section
    tk = min(kdim, CAST_BLOCK_ELEMS // tile)
    base = col0 // tile

    def src(p, kk):
        return (layer, kk, base + (p % sections) * tiles_per_section + p // sections)

    if tile_major:
        out_shape = jax.ShapeDtypeStruct((n_tiles, kdim, tile), BF16)
        out_spec = pl.BlockSpec((None, tk, tile), lambda p, kk: (p, kk, 0))
    else:
        out_shape = jax.ShapeDtypeStruct((kdim, n_tiles * tile), BF16)
        out_spec = pl.BlockSpec((tk, tile), lambda p, kk: (kk, p))
    return pl.pallas_call(
        _cast_kernel,
        grid=(n_tiles, kdim // tk),
        in_specs=[pl.BlockSpec((None, tk, tile), src)],
        out_specs=out_spec,
        out_shape=out_shape,
        compiler_params=_params(2, 6 * CAST_BLOCK_ELEMS * 4),
        name="cast_weight",
    )(w)


def _layer_weights(layer, w_in, w_ret_out, w_conv_out, w_out, w_up, w_down):
    d = w_in.shape[1]
    dr = w_ret_out.shape[1]
    dc = w_conv_out.shape[1]
    f = w_up.shape[2]
    hd = dr // RET_HEADS
    cast = functools.partial(_cast_regroup, layer=layer)
    return dict(
        w_ret=cast(w_in, col0=0, sections=4, width=dr, tile=hd),
        w_cbz=cast(w_in, col0=4 * dr, sections=3, width=dc, tile=_conv_tile(dc)),
        w_gate=cast(w_in, col0=4 * dr + 3 * dc, sections=2, width=d, tile=_merge_tile(d)),
        w_ro=cast(w_ret_out, col0=0, sections=1, width=d, tile=_merge_tile(d)),
        w_co=cast(w_conv_out, col0=0, sections=1, width=d, tile=_merge_tile(d)),
        w_o=cast(w_out, col0=0, sections=1, width=d, tile=min(d, COL_TILE_RES_O), tile_major=True),
        w_up=cast(w_up, col0=0, sections=1, width=f, tile=min(f, COL_TILE_UP)),
        w_down=cast(w_down, col0=0, sections=1, width=d, tile=min(d, COL_TILE_RES_DOWN),
                    tile_major=True),
    )


def _rope_tables(pos, hd):
    inv_freq = ROPE_BASE ** (-jnp.arange(0, hd, 2, dtype=F32) / hd)
    ang = pos[:, None] * inv_freq[None, :]
    return jnp.cos(ang), jnp.sin(ang)


def kernel(x, meta_tokens, norm_mix_g, w_in, conv_w, w_ret_out, w_conv_out, w_out,
           norm_mlp_g, w_up, w_down, final_norm_g):
    batch, seq, d = x.shape
    depth = w_in.shape[0]
    n_meta = meta_tokens.shape[0]
    dr = d // 2
    dc = d // 2
    hd = dr // RET_HEADS
    assert seq % CHUNK == 0 and n_meta <= CHUNK
    meta_pad = CHUNK - n_meta

    layers = [_layer_weights(i, w_in, w_ret_out, w_conv_out, w_out, w_up, w_down)
              for i in range(depth)]
    log_gamma = jnp.asarray(
        np.log1p(-np.exp2(-5.0 - np.arange(RET_HEADS, dtype=np.float64))), F32)
    cos_x, sin_x = _rope_tables(jnp.arange(seq, dtype=F32) + n_meta, hd)
    cos_m, sin_m = _rope_tables(jnp.arange(CHUNK, dtype=F32) - meta_pad, hd)

    def mixer(u, lw, i, cos, sin, state0, halo, nb, keep):
        ret = _retention(u, lw["w_ret"], cos, sin, state0, log_gamma, batch=nb, emit_state=keep)
        cnv = _conv_mixer(u, lw["w_cbz"], conv_w[i], halo, batch=nb, emit_tail=keep)
        return ret, cnv

    def rest_of_layer(h, u, a1, b1, lw, i):
        merged = _merge(a1, b1, u, lw["w_ro"], lw["w_co"], lw["w_gate"])
        h = _residual(merged, lw["w_o"], h, k_splits=1)
        hid = _up(_rmsnorm(h, norm_mlp_g[i], BF16), lw["w_up"])
        return _residual(hid, lw["w_down"], h, k_splits=K_SPLITS_DOWN)

    hm = jnp.concatenate([jnp.zeros((meta_pad, d), F32), meta_tokens.astype(F32)], axis=0)
    zero_state = jnp.zeros((RET_HEADS, hd, hd), F32)
    zero_halo = jnp.zeros((SUBLANES, dc), F32)
    states, halos = [], []
    for i, lw in enumerate(layers):
        um = _rmsnorm(hm, norm_mix_g[i], BF16)
        (a1, state), (b1, tail) = mixer(um, lw, i, cos_m, sin_m, zero_state, zero_halo, 1, True)
        states.append(state)
        halos.append(tail)
        if i + 1 < depth:
            hm = rest_of_layer(hm, um, a1, b1, lw, i)

    h = x.reshape(batch * seq, d)
    for i, lw in enumerate(layers):
        u = _rmsnorm(h, norm_mix_g[i], BF16)
        a1, b1 = mixer(u, lw, i, cos_x, sin_x, states[i], halos[i], batch, False)
        h = rest_of_layer(h, u, a1, b1, lw, i)
    return _rmsnorm(h, final_norm_g, F32).reshape(batch, seq, d)
```

```python
import functools
from typing import NamedTuple

import numpy as np
import jax
import jax.numpy as jnp
from jax import lax
from jax.experimental import pallas as pl
from jax.experimental.pallas import tpu as pltpu

N_META = 16
RET_HEADS = 8
CONV_WIDTH = 3
CHUNK = 128
ROPE_BASE = 10000.0
EPS = 1e-6

V7X_VMEM_BYTES = 64 * 1024 * 1024
SUBLANES = 8
LANES = 128
MIB = 1024 * 1024

F32 = jnp.float32
BF16 = jnp.bfloat16

ROW_TILE_NORM = 256
ROW_TILE_RET = 1024
ROW_TILE_CONV = 1024
COL_TILE_CONV = 256
ROW_TILE_MERGE = 512
COL_TILE_MERGE = 512
ROW_TILE_RES = 1024
COL_TILE_RES_O = 512
COL_TILE_RES_DOWN = 256
K_SPLITS_DOWN = 2
ROW_TILE_UP = 1024
COL_TILE_UP = 1024
CAST_BLOCK_ELEMS = 1024 * 1024
SIDE_BLOCK_ELEMS = 512 * 1024
SMALL_BLOCK_ELEMS = 256 * 1024


def _dot(a, b):
    return jnp.dot(a, b, preferred_element_type=F32)


def _row_tile(m, want):
    return min(m, want)


def _conv_tile(dc):
    return min(dc, COL_TILE_CONV)


def _merge_tile(d):
    return min(d, COL_TILE_MERGE)


def _inv_rms(ssq_ref, d):
    return lax.rsqrt(ssq_ref[...] / d + EPS)


class _Cast(NamedTuple):
    w: jax.Array
    layer: int
    col0: int
    sections: int
    width: int
    tile: int
    tile_major: bool
    block_elems: int


def _cast_geometry(c):
    kdim = c.w.shape[1]
    tiles_per_section = c.width // c.tile
    n_tiles = c.sections * tiles_per_section
    tk = min(kdim, c.block_elems // c.tile)
    return kdim, tiles_per_section, n_tiles, tk, kdim // tk


def _cast_specs(c, step_of):
    kdim, tiles_per_section, n_tiles, tk, kblocks = _cast_geometry(c)
    n_blocks = n_tiles * kblocks
    base = c.col0 // c.tile

    def tile_and_kblock(*idx):
        t = jnp.minimum(step_of(*idx), n_blocks - 1)
        return t // kblocks, t % kblocks

    def src(*idx):
        p, kk = tile_and_kblock(*idx)
        return c.layer, kk, base + (p % c.sections) * tiles_per_section + p // c.sections

    if c.tile_major:
        shape = (n_tiles, kdim, c.tile)
        block = (None, tk, c.tile)

        def dst(*idx):
            p, kk = tile_and_kblock(*idx)
            return p, kk, 0
    else:
        shape = (kdim, n_tiles * c.tile)
        block = (tk, c.tile)

        def dst(*idx):
            p, kk = tile_and_kblock(*idx)
            return kk, p

    return (pl.BlockSpec((None, tk, c.tile), src), pl.BlockSpec(block, dst),
            jax.ShapeDtypeStruct(shape, BF16), n_blocks)


def _call(body, *, grid, in_specs, out_specs, out_shape, scratch, vmem, name, args, casts=()):
    n_in, n_out, n_cast = len(in_specs), len(out_specs), len(casts)

    def step_of(*idx):
        t = idx[0]
        for size, i in zip(grid[1:], idx[1:]):
            t = t * size + i
        return t

    specs = [_cast_specs(c, step_of) for c in casts]
    n_steps = int(np.prod(grid))
    for c, s in zip(casts, specs):
        assert s[3] <= n_steps, (name, s[3], n_steps)
        vmem += 2 * c.block_elems * (4 + 2)

    def kernel(*refs):
        ins = refs[:n_in]
        cast_in = refs[n_in:n_in + n_cast]
        outs = refs[n_in + n_cast:n_in + n_cast + n_out]
        cast_out = refs[n_in + n_cast + n_out:n_in + 2 * n_cast + n_out]
        body(*ins, *outs, *refs[n_in + 2 * n_cast + n_out:])
        for x_ref, o_ref in zip(cast_in, cast_out):
            o_ref[...] = x_ref[...].astype(o_ref.dtype)

    assert vmem <= V7X_VMEM_BYTES - 4 * MIB, (name, vmem)
    res = pl.pallas_call(
        kernel,
        grid=grid,
        in_specs=[*in_specs, *[s[0] for s in specs]],
        out_specs=[*out_specs, *[s[1] for s in specs]],
        out_shape=[*out_shape, *[s[2] for s in specs]],
        scratch_shapes=list(scratch),
        compiler_params=pltpu.CompilerParams(
            dimension_semantics=("arbitrary",) * len(grid), vmem_limit_bytes=int(vmem)),
        name=name,
    )(*args, *[c.w for c in casts])
    return list(res[:n_out]), list(res[n_out:])


def _cast_now(c):
    n_tiles, kblocks = _cast_geometry(c)[2], _cast_geometry(c)[4]
    _, (w,) = _call(lambda: None, grid=(n_tiles * kblocks,), in_specs=[], out_specs=[],
                    out_shape=[], scratch=(), vmem=4 * MIB, name="cast_weight", args=(),
                    casts=(c,))
    return w


def _prep_kernel(x_ref, g_ref, hg_ref, ssq_ref):
    x = x_ref[...]
    hg_ref[...] = (x * g_ref[...]).astype(hg_ref.dtype)
    ssq_ref[...] = jnp.sum(x * x, axis=-1, keepdims=True)


def _prep(x, g):
    m, d = x.shape
    tm = _row_tile(m, ROW_TILE_NORM)
    (hg, ssq), _ = _call(
        _prep_kernel, grid=(m // tm,),
        in_specs=[pl.BlockSpec((tm, d), lambda i: (i, 0)), pl.BlockSpec((1, d), lambda i: (0, 0))],
        out_specs=[pl.BlockSpec((tm, d), lambda i: (i, 0)), pl.BlockSpec((tm, 1), lambda i: (i, 0))],
        out_shape=[jax.ShapeDtypeStruct((m, d), BF16), jax.ShapeDtypeStruct((m, 1), F32)],
        scratch=(), vmem=6 * tm * d * 4 + 4 * MIB, name="norm_inputs", args=(x, g.reshape(1, d)))
    return hg, ssq


def _rmsnorm_kernel(x_ref, g_ref, o_ref):
    x = x_ref[...]
    y = x * lax.rsqrt(jnp.mean(x * x, axis=-1, keepdims=True) + EPS)
    o_ref[...] = (y * g_ref[...]).astype(o_ref.dtype)


def _rmsnorm(x, g):
    m, d = x.shape
    tm = _row_tile(m, ROW_TILE_NORM)
    (y,), _ = _call(
        _rmsnorm_kernel, grid=(m // tm,),
        in_specs=[pl.BlockSpec((tm, d), lambda i: (i, 0)), pl.BlockSpec((1, d), lambda i: (0, 0))],
        out_specs=[pl.BlockSpec((tm, d), lambda i: (i, 0))],
        out_shape=[jax.ShapeDtypeStruct((m, d), F32)],
        scratch=(), vmem=7 * tm * d * 4 + 4 * MIB, name="rmsnorm", args=(x, g.reshape(1, d)))
    return y


def _retention_kernel(logg_ref, hg_ref, ssq_ref, w_ref, cos_ref, sin_ref, s0_ref, *rest,
                      tm, hd, emit_state):
    if emit_state:
        a_ref, sout_ref, state_ref, q_s, k_s, v_s, g_s = rest
    else:
        a_ref, state_ref, q_s, k_s, v_s, g_s = rest
    rb = pl.program_id(1)
    h = pl.program_id(2)
    half = hd // 2

    @pl.when(rb == 0)
    def _():
        state_ref[h] = s0_ref[...]

    proj = _dot(hg_ref[...], w_ref[...]) * _inv_rms(ssq_ref, hg_ref.shape[1])
    cos = cos_ref[...]
    sin = sin_ref[...]

    def rot(t):
        t1, t2 = t[:, :half], t[:, half:]
        return jnp.concatenate([t1 * cos - t2 * sin, t1 * sin + t2 * cos], axis=-1)

    q_s[...] = rot(proj[:, 0 * hd:1 * hd]).astype(BF16)
    k_s[...] = rot(proj[:, 1 * hd:2 * hd]) * (hd ** -0.5)
    v_s[...] = proj[:, 2 * hd:3 * hd].astype(BF16)
    g = proj[:, 3 * hd:4 * hd]
    g_s[...] = g * jax.nn.sigmoid(g)

    lg = logg_ref[h]
    ri = lax.broadcasted_iota(jnp.int32, (CHUNK, CHUNK), 0)
    ci = lax.broadcasted_iota(jnp.int32, (CHUNK, CHUNK), 1)
    rel = (ri - ci).astype(F32)
    inner_decay = jnp.where(rel >= 0, jnp.exp(lg * jnp.maximum(rel, 0.0)), 0.0)
    jj = lax.broadcasted_iota(jnp.int32, (CHUNK, 1), 0).astype(F32)
    q_decay = jnp.exp(lg * (jj + 1.0))
    k_decay = jnp.exp(lg * (CHUNK - 1.0 - jj))
    chunk_decay = jnp.exp(jnp.full((1, hd), lg * CHUNK, F32))

    def chunk(c, carry):
        rows = pl.ds(pl.multiple_of(c * CHUNK, CHUNK), CHUNK)
        qc = q_s[rows, :]
        kc = k_s[rows, :]
        vc = v_s[rows, :]
        st = state_ref[h]
        scores = lax.dot_general(qc, kc.astype(BF16), (((1,), (1,)), ((), ())),
                                 preferred_element_type=F32) * inner_decay
        inner = _dot(scores.astype(BF16), vc)
        cross = _dot(qc, st.astype(BF16)) * q_decay
        state_ref[h] = st * chunk_decay + lax.dot_general(
            (kc * k_decay).astype(BF16), vc, (((0,), (0,)), ((), ())),
            preferred_element_type=F32)
        o = inner + cross
        o = o * lax.rsqrt(jnp.mean(o * o, axis=-1, keepdims=True) + EPS)
        a_ref[rows, :] = (o * g_s[rows, :]).astype(a_ref.dtype)
        return carry

    lax.fori_loop(0, tm // CHUNK, chunk, 0, unroll=True)

    if emit_state:
        sout_ref[...] = state_ref[h]


def _retention(hg, ssq, w_ret, cos, sin, state0, log_gamma, *, batch, emit_state, casts=()):
    m, d = hg.shape
    lb = m // batch
    heads = RET_HEADS
    hd = w_ret.shape[1] // (4 * heads)
    tm = _row_tile(lb, ROW_TILE_RET)
    nrb = lb // tm
    out_shape = [jax.ShapeDtypeStruct((m, heads * hd), BF16)]
    out_specs = [pl.BlockSpec((tm, hd), lambda b, r, h: (b * nrb + r, h))]
    if emit_state:
        out_shape.append(jax.ShapeDtypeStruct((heads, hd, hd), F32))
        out_specs.append(pl.BlockSpec((None, hd, hd), lambda b, r, h: (h, 0, 0)))
    vmem = (2 * tm * d * 2 + 2 * d * 4 * hd * 2 + 3 * tm * 4 * hd * 4
            + heads * hd * hd * 4 + 6 * MIB)
    return _call(
        functools.partial(_retention_kernel, tm=tm, hd=hd, emit_state=emit_state),
        grid=(batch, nrb, heads),
        in_specs=[
            pl.BlockSpec(memory_space=pltpu.SMEM),
            pl.BlockSpec((tm, d), lambda b, r, h: (b * nrb + r, 0)),
            pl.BlockSpec((tm, 1), lambda b, r, h: (b * nrb + r, 0)),
            pl.BlockSpec((d, 4 * hd), lambda b, r, h: (0, h)),
            pl.BlockSpec((tm, hd // 2), lambda b, r, h: (r, 0)),
            pl.BlockSpec((tm, hd // 2), lambda b, r, h: (r, 0)),
            pl.BlockSpec((None, hd, hd), lambda b, r, h: (h, 0, 0)),
        ],
        out_specs=out_specs, out_shape=out_shape,
        scratch=[
            pltpu.VMEM((heads, hd, hd), F32),
            pltpu.VMEM((tm, hd), BF16),
            pltpu.VMEM((tm, hd), F32),
            pltpu.VMEM((tm, hd), BF16),
            pltpu.VMEM((tm, hd), F32),
        ],
        vmem=vmem, name="retention",
        args=(log_gamma, hg, ssq, w_ret, cos, sin, state0), casts=casts)


def _conv_kernel(hg_ref, ssq_ref, w_ref, cw_ref, halo_ref, *rest, tm, tn, nrb, emit_tail):
    if emit_tail:
        b1_ref, tail_ref, carry_ref = rest
    else:
        b1_ref, carry_ref = rest
    i = pl.program_id(0)
    j = pl.program_id(1)

    @pl.when(i % nrb == 0)
    def _():
        carry_ref[j] = halo_ref[...]

    proj = _dot(hg_ref[...], w_ref[...]) * _inv_rms(ssq_ref, hg_ref.shape[1])
    zc = proj[:, :tn] * proj[:, 2 * tn:]
    prev = carry_ref[j]
    p1 = prev[SUBLANES - 1:SUBLANES, :]
    p2 = prev[SUBLANES - 2:SUBLANES - 1, :]
    row = lax.broadcasted_iota(jnp.int32, zc.shape, 0)
    z1 = jnp.where(row == 0, p1, pltpu.roll(zc, 1, 0))
    z2 = jnp.where(row == 0, p2, jnp.where(row == 1, p1, pltpu.roll(zc, 2, 0)))
    cw = cw_ref[...]
    conv = cw[0:1, :] * z2 + cw[1:2, :] * z1 + cw[2:3, :] * zc
    b1_ref[...] = (proj[:, tn:2 * tn] * conv).astype(b1_ref.dtype)
    tail = zc[tm - SUBLANES:, :]
    carry_ref[j] = tail
    if emit_tail:
        tail_ref[...] = tail


def _conv_mixer(hg, ssq, w_cbz, conv_w, halo, *, batch, emit_tail, casts=()):
    m, d = hg.shape
    lb = m // batch
    dc = conv_w.shape[1]
    tn = _conv_tile(dc)
    tm = _row_tile(lb, ROW_TILE_CONV)
    nrb = lb // tm
    ni, nj = m // tm, dc // tn
    out_shape = [jax.ShapeDtypeStruct((m, dc), BF16)]
    out_specs = [pl.BlockSpec((tm, tn), lambda i, j: (i, j))]
    if emit_tail:
        out_shape.append(jax.ShapeDtypeStruct((ni * SUBLANES, dc), F32))
        out_specs.append(pl.BlockSpec((SUBLANES, tn), lambda i, j: (i, j)))
    vmem = 2 * tm * d * 2 + 2 * d * 3 * tn * 2 + 3 * tm * 3 * tn * 4 + 4 * MIB
    return _call(
        functools.partial(_conv_kernel, tm=tm, tn=tn, nrb=nrb, emit_tail=emit_tail),
        grid=(ni, nj),
        in_specs=[
            pl.BlockSpec((tm, d), lambda i, j: (i, 0)),
            pl.BlockSpec((tm, 1), lambda i, j: (i, 0)),
            pl.BlockSpec((d, 3 * tn), lambda i, j: (0, j)),
            pl.BlockSpec((CONV_WIDTH, tn), lambda i, j: (0, j)),
            pl.BlockSpec((SUBLANES, tn), lambda i, j: (0, j)),
        ],
        out_specs=out_specs, out_shape=out_shape,
        scratch=[pltpu.VMEM((nj, SUBLANES, tn), F32)],
        vmem=vmem, name="conv_mixer", args=(hg, ssq, w_cbz, conv_w, halo), casts=casts)


def _merge_kernel(a_ref, b_ref, hg_ref, ssq_ref, wro_ref, wco_ref, wg_ref, o_ref, *, tn):
    ya = _dot(a_ref[...], wro_ref[...])
    yb = _dot(b_ref[...], wco_ref[...])
    gates = _dot(hg_ref[...], wg_ref[...]) * _inv_rms(ssq_ref, hg_ref.shape[1])
    o_ref[...] = (jax.nn.sigmoid(gates[:, :tn]) * ya
                  + jax.nn.sigmoid(gates[:, tn:]) * yb).astype(o_ref.dtype)


def _merge(a1, b1, hg, ssq, w_ro, w_co, w_gate, *, casts=()):
    m, d = hg.shape
    dr = a1.shape[1]
    dc = b1.shape[1]
    tn = _merge_tile(d)
    tm = _row_tile(m, ROW_TILE_MERGE)
    vmem = (2 * tm * (dr + dc + d) * 2 + 2 * (dr + dc + 2 * d) * tn * 2
            + 8 * tm * tn * 4 + 2 * MIB)
    (merged,), cast_out = _call(
        functools.partial(_merge_kernel, tn=tn),
        grid=(m // tm, d // tn),
        in_specs=[
            pl.BlockSpec((tm, dr), lambda i, j: (i, 0)),
            pl.BlockSpec((tm, dc), lambda i, j: (i, 0)),
            pl.BlockSpec((tm, d), lambda i, j: (i, 0)),
            pl.BlockSpec((tm, 1), lambda i, j: (i, 0)),
            pl.BlockSpec((dr, tn), lambda i, j: (0, j)),
            pl.BlockSpec((dc, tn), lambda i, j: (0, j)),
            pl.BlockSpec((d, 2 * tn), lambda i, j: (0, j)),
        ],
        out_specs=[pl.BlockSpec((tm, tn), lambda i, j: (i, j))],
        out_shape=[jax.ShapeDtypeStruct((m, d), BF16)],
        scratch=(), vmem=vmem, name="merge",
        args=(a1, b1, hg, ssq, w_ro, w_co, w_gate), casts=casts)
    return merged, cast_out


def _residual_kernel(a_ref, w_ref, h_ref, *rest, nt, emit_scaled):
    if emit_scaled:
        g_ref, o_ref, hg_ref, ssq_ref, acc_ref = rest
    else:
        (o_ref,) = rest
    j = pl.program_id(1)
    if emit_scaled:
        @pl.when(j == 0)
        def _():
            acc_ref[...] = jnp.zeros_like(acc_ref)

    hn = h_ref[...] + _dot(a_ref[...], w_ref[...])
    o_ref[...] = hn
    if emit_scaled:
        hg_ref[...] = (hn * g_ref[...]).astype(hg_ref.dtype)
        sq = hn * hn
        part = sq[:, :LANES]
        for c in range(1, sq.shape[1] // LANES):
            part = part + sq[:, c * LANES:(c + 1) * LANES]
        acc_ref[...] += part

        @pl.when(j == nt - 1)
        def _():
            ssq_ref[...] = jnp.sum(acc_ref[...], axis=-1, keepdims=True)


def _residual(a, w_tiles, h, g_next, *, k_splits, casts=()):
    m, kdim = a.shape
    nt, _, tn = w_tiles.shape
    d = nt * tn
    kb = kdim // k_splits
    tm = _row_tile(m, ROW_TILE_RES)
    casts = casts or ((),) * k_splits
    hg = ssq = None
    cast_out = []
    for s in range(k_splits):
        emit_scaled = g_next is not None and s == k_splits - 1
        in_specs = [
            pl.BlockSpec((tm, kb), lambda i, j, s=s: (i, s)),
            pl.BlockSpec((None, kb, tn), lambda i, j, s=s: (j, s, 0)),
            pl.BlockSpec((tm, tn), lambda i, j: (i, j)),
        ]
        out_specs = [pl.BlockSpec((tm, tn), lambda i, j: (i, j))]
        out_shape = [jax.ShapeDtypeStruct((m, d), F32)]
        args = [a, w_tiles, h]
        scratch = []
        vmem = 2 * tm * kb * 2 + 2 * kb * tn * 2 + 5 * tm * tn * 4 + 4 * MIB
        if emit_scaled:
            in_specs.append(pl.BlockSpec((1, tn), lambda i, j: (0, j)))
            args.append(g_next.reshape(1, d))
            out_specs += [pl.BlockSpec((tm, tn), lambda i, j: (i, j)),
                          pl.BlockSpec((tm, 1), lambda i, j: (i, 0))]
            out_shape += [jax.ShapeDtypeStruct((m, d), BF16), jax.ShapeDtypeStruct((m, 1), F32)]
            scratch.append(pltpu.VMEM((tm, LANES), F32))
            vmem += 2 * tm * tn * 2 + 4 * tm * LANES * 4
        outs, cw = _call(
            functools.partial(_residual_kernel, nt=nt, emit_scaled=emit_scaled),
            grid=(m // tm, nt), in_specs=in_specs, out_specs=out_specs, out_shape=out_shape,
            scratch=scratch, vmem=vmem, name="residual", args=args, casts=casts[s])
        cast_out += cw
        h = outs[0]
        if emit_scaled:
            hg, ssq = outs[1], outs[2]
    return (h, hg, ssq), cast_out


def _up_kernel(hg_ref, ssq_ref, w_ref, o_ref):
    t = jnp.maximum(_dot(hg_ref[...], w_ref[...]) * _inv_rms(ssq_ref, hg_ref.shape[1]), 0.0)
    o_ref[...] = (t * t).astype(o_ref.dtype)


def _up(hg, ssq, w_up, *, casts=()):
    m, d = hg.shape
    f = w_up.shape[1]
    tm = _row_tile(m, ROW_TILE_UP)
    tn = min(f, COL_TILE_UP)
    vmem = 2 * tm * d * 2 + 2 * d * tn * 2 + 2 * tm * tn * 2 + 3 * tm * tn * 4 + 2 * MIB
    (hid,), cast_out = _call(
        _up_kernel, grid=(m // tm, f // tn),
        in_specs=[pl.BlockSpec((tm, d), lambda i, j: (i, 0)),
                  pl.BlockSpec((tm, 1), lambda i, j: (i, 0)),
                  pl.BlockSpec((d, tn), lambda i, j: (0, j))],
        out_specs=[pl.BlockSpec((tm, tn), lambda i, j: (i, j))],
        out_shape=[jax.ShapeDtypeStruct((m, f), BF16)],
        scratch=(), vmem=vmem, name="mlp_up", args=(hg, ssq, w_up), casts=casts)
    return hid, cast_out


def _rope_tables(pos, hd):
    inv_freq = ROPE_BASE ** (-jnp.arange(0, hd, 2, dtype=F32) / hd)
    ang = pos[:, None] * inv_freq[None, :]
    return jnp.cos(ang), jnp.sin(ang)


def kernel(x, meta_tokens, norm_mix_g, w_in, conv_w, w_ret_out, w_conv_out, w_out,
           norm_mlp_g, w_up, w_down, final_norm_g):
    batch, seq, d = x.shape
    depth = w_in.shape[0]
    n_meta = meta_tokens.shape[0]
    dr = w_ret_out.shape[1]
    dc = w_conv_out.shape[1]
    f = w_up.shape[2]
    hd = dr // RET_HEADS
    assert seq % CHUNK == 0 and n_meta <= CHUNK
    meta_pad = CHUNK - n_meta

    def cast(name, layer, block_elems):
        spec = {
            "w_ret": (w_in, 0, 4, dr, hd, False),
            "w_cbz": (w_in, 4 * dr, 3, dc, _conv_tile(dc), False),
            "w_gate": (w_in, 4 * dr + 3 * dc, 2, d, _merge_tile(d), False),
            "w_ro": (w_ret_out, 0, 1, d, _merge_tile(d), False),
            "w_co": (w_conv_out, 0, 1, d, _merge_tile(d), False),
            "w_o": (w_out, 0, 1, d, min(d, COL_TILE_RES_O), True),
            "w_up": (w_up, 0, 1, f, min(f, COL_TILE_UP), False),
            "w_down": (w_down, 0, 1, d, min(d, COL_TILE_RES_DOWN), True),
        }[name]
        return _Cast(spec[0], layer, *spec[1:], block_elems)

    lw = [dict() for _ in range(depth)]
    for name in ("w_ret", "w_cbz", "w_gate", "w_ro", "w_co", "w_o"):
        lw[0][name] = _cast_now(cast(name, 0, CAST_BLOCK_ELEMS))

    def riding(layer, names, block_elems=SIDE_BLOCK_ELEMS):
        return tuple(cast(n, layer, block_elems) for n in names) if layer < depth else ()

    def keep(layer, names, cast_out):
        if layer < depth:
            lw[layer].update(zip(names, cast_out))

    log_gamma = jnp.asarray(
        np.log1p(-np.exp2(-5.0 - np.arange(RET_HEADS, dtype=np.float64))), F32)
    cos_x, sin_x = _rope_tables(jnp.arange(seq, dtype=F32) + n_meta, hd)
    cos_m, sin_m = _rope_tables(jnp.arange(CHUNK, dtype=F32) - meta_pad, hd)
    zero_state = jnp.zeros((RET_HEADS, hd, hd), F32)
    zero_halo = jnp.zeros((SUBLANES, dc), F32)

    def rest_of_layer(i, h, hg, ssq, a1, b1, carry_casts):
        w = lw[i]
        nxt = i + 1
        names = dict(merge=("w_gate",), o=("w_ret", "w_cbz"), up=("w_up",), down0=("w_down",),
                     down1=("w_o",))
        c = {k: riding(nxt, v) if carry_casts else () for k, v in names.items()}
        merged, out = _merge(a1, b1, hg, ssq, w["w_ro"], w["w_co"], w["w_gate"], casts=c["merge"])
        keep(nxt, names["merge"], out)
        (h, hg, ssq), out = _residual(merged, w["w_o"], h, norm_mlp_g[i], k_splits=1,
                                      casts=(c["o"],))
        keep(nxt, names["o"], out)
        hid, out = _up(hg, ssq, w["w_up"], casts=c["up"])
        keep(nxt, names["up"], out)
        g_next = norm_mix_g[nxt] if nxt < depth else None
        (h, hg, ssq), out = _residual(hid, w["w_down"], h, g_next, k_splits=K_SPLITS_DOWN,
                                      casts=(c["down0"], c["down1"]))
        keep(nxt, names["down0"] + names["down1"], out)
        return h, hg, ssq

    hm = jnp.concatenate([jnp.zeros((meta_pad, d), F32), meta_tokens.astype(F32)], axis=0)
    hgm, ssqm = _prep(hm, norm_mix_g[0])
    h = x.reshape(batch * seq, d)
    hg, ssq = _prep(h, norm_mix_g[0])
    for i in range(depth):
        w = lw[i]
        (a1m, state), _ = _retention(hgm, ssqm, w["w_ret"], cos_m, sin_m, zero_state, log_gamma,
                                     batch=1, emit_state=True)
        (b1m, halo), _ = _conv_mixer(hgm, ssqm, w["w_cbz"], conv_w[i], zero_halo, batch=1,
                                     emit_tail=True)
        first = ("w_up",) if i == 0 else ()
        (a1,), out = _retention(hg, ssq, w["w_ret"], cos_x, sin_x, state, log_gamma, batch=batch,
                                emit_state=False, casts=riding(i, first))
        keep(i, first, out)
        first = ("w_down",) if i == 0 else ()
        small = ("w_ro", "w_co")
        (b1,), out = _conv_mixer(hg, ssq, w["w_cbz"], conv_w[i], halo, batch=batch, emit_tail=False,
                                 casts=riding(i, first) + riding(i + 1, small, SMALL_BLOCK_ELEMS))
        keep(i, first, out[:len(first)])
        keep(i + 1, small, out[len(first):])
        if i + 1 < depth:
            hm, hgm, ssqm = rest_of_layer(i, hm, hgm, ssqm, a1m, b1m, False)
        h, hg, ssq = rest_of_layer(i, h, hg, ssq, a1, b1, True)
    return _rmsnorm(h, final_norm_g).reshape(batch, seq, d)
```

```python
import functools
from typing import NamedTuple

import numpy as np
import jax
import jax.numpy as jnp
from jax import lax
from jax.experimental import pallas as pl
from jax.experimental.pallas import tpu as pltpu

N_META = 16
RET_HEADS = 8
CONV_WIDTH = 3
CHUNK = 128
ROPE_BASE = 10000.0
EPS = 1e-6

V7X_VMEM_BYTES = 64 * 1024 * 1024
SUBLANES = 8
LANES = 128
MIB = 1024 * 1024

F32 = jnp.float32
BF16 = jnp.bfloat16

ROW_TILE_NORM = 256
ROW_TILE_RET = 1024
ROW_GROUPS_RET = 2
ROW_TILE_CONV = 1024
COL_TILE_CONV = 256
ROW_TILE_MERGE = 512
COL_TILE_MERGE = 512
ROW_TILE_RES = 1024
COL_TILE_RES_O = 512
COL_TILE_RES_DOWN = 256
K_SPLITS_DOWN = 2
ROW_TILE_UP = 1024
COL_TILE_UP = 1024
CAST_BLOCK_ELEMS = 1024 * 1024
MIN_CAST_BLOCK_ELEMS = 64 * 1024


def _dot(a, b):
    return jnp.dot(a, b, preferred_element_type=F32)


def _row_tile(m, want):
    return min(m, want)


def _conv_tile(dc):
    return min(dc, COL_TILE_CONV)


def _merge_tile(d):
    return min(d, COL_TILE_MERGE)


def _inv_rms(ssq, d):
    return lax.rsqrt(ssq / d + EPS)


class _Cast(NamedTuple):
    w: jax.Array
    layer: int
    col0: int
    sections: int
    width: int
    tile: int
    tile_major: bool
    block_elems: int = 0


def _cast_geometry(c):
    kdim = c.w.shape[1]
    tiles_per_section = c.width // c.tile
    n_tiles = c.sections * tiles_per_section
    tk = min(kdim, c.block_elems // c.tile)
    return kdim, tiles_per_section, n_tiles, tk, kdim // tk


def _fit_blocks(c, n_steps):
    c = c._replace(block_elems=MIN_CAST_BLOCK_ELEMS)
    while _cast_geometry(c)[2] * _cast_geometry(c)[4] > n_steps:
        assert _cast_geometry(c)[3] < c.w.shape[1], (c.block_elems, n_steps)
        c = c._replace(block_elems=2 * c.block_elems)
    return c


def _cast_specs(c, step_of):
    kdim, tiles_per_section, n_tiles, tk, kblocks = _cast_geometry(c)
    n_blocks = n_tiles * kblocks
    base = c.col0 // c.tile

    def tile_and_kblock(*idx):
        t = jnp.minimum(step_of(*idx), n_blocks - 1)
        return t // kblocks, t % kblocks

    def src(*idx):
        p, kk = tile_and_kblock(*idx)
        return c.layer, kk, base + (p % c.sections) * tiles_per_section + p // c.sections

    if c.tile_major:
        shape = (n_tiles, kdim, c.tile)
        block = (None, tk, c.tile)

        def dst(*idx):
            p, kk = tile_and_kblock(*idx)
            return p, kk, 0
    else:
        shape = (kdim, n_tiles * c.tile)
        block = (tk, c.tile)

        def dst(*idx):
            p, kk = tile_and_kblock(*idx)
            return kk, p

    return (pl.BlockSpec((None, tk, c.tile), src), pl.BlockSpec(block, dst),
            jax.ShapeDtypeStruct(shape, BF16), n_blocks)


def _call(body, *, grid, in_specs, out_specs, out_shape, scratch, vmem, name, args, casts=()):
    n_in, n_out, n_cast = len(in_specs), len(out_specs), len(casts)

    def step_of(*idx):
        t = idx[0]
        for size, i in zip(grid[1:], idx[1:]):
            t = t * size + i
        return t

    n_steps = int(np.prod(grid))
    casts = [c if c.block_elems else _fit_blocks(c, n_steps) for c in casts]
    specs = [_cast_specs(c, step_of) for c in casts]
    for c, s in zip(casts, specs):
        assert s[3] <= n_steps, (name, s[3], n_steps)
        vmem += 2 * c.block_elems * (4 + 2)

    def kernel(*refs):
        ins = refs[:n_in]
        cast_in = refs[n_in:n_in + n_cast]
        outs = refs[n_in + n_cast:n_in + n_cast + n_out]
        cast_out = refs[n_in + n_cast + n_out:n_in + 2 * n_cast + n_out]
        for x_ref, o_ref in zip(cast_in, cast_out):
            o_ref[...] = x_ref[...].astype(o_ref.dtype)
        body(*ins, *outs, *refs[n_in + 2 * n_cast + n_out:])

    assert vmem <= V7X_VMEM_BYTES - 4 * MIB, (name, vmem)
    res = pl.pallas_call(
        kernel,
        grid=grid,
        in_specs=[*in_specs, *[s[0] for s in specs]],
        out_specs=[*out_specs, *[s[1] for s in specs]],
        out_shape=[*out_shape, *[s[2] for s in specs]],
        scratch_shapes=list(scratch),
        compiler_params=pltpu.CompilerParams(
            dimension_semantics=("arbitrary",) * len(grid), vmem_limit_bytes=int(vmem)),
        name=name,
    )(*args, *[c.w for c in casts])
    return list(res[:n_out]), list(res[n_out:])


def _cast_now(c):
    c = c._replace(block_elems=CAST_BLOCK_ELEMS)
    n_tiles, kblocks = _cast_geometry(c)[2], _cast_geometry(c)[4]
    _, (w,) = _call(lambda: None, grid=(n_tiles * kblocks,), in_specs=[], out_specs=[],
                    out_shape=[], scratch=(), vmem=4 * MIB, name="cast_weight", args=(),
                    casts=(c,))
    return w


def _prep_kernel(x_ref, g_ref, hg_ref, ssq_ref):
    x = x_ref[...]
    hg_ref[...] = (x * g_ref[...]).astype(hg_ref.dtype)
    ssq_ref[...] = jnp.sum(x * x, axis=-1, keepdims=True)


def _prep(x, g):
    m, d = x.shape
    tm = _row_tile(m, ROW_TILE_NORM)
    (hg, ssq), _ = _call(
        _prep_kernel, grid=(m // tm,),
        in_specs=[pl.BlockSpec((tm, d), lambda i: (i, 0)), pl.BlockSpec((1, d), lambda i: (0, 0))],
        out_specs=[pl.BlockSpec((tm, d), lambda i: (i, 0)), pl.BlockSpec((tm, 1), lambda i: (i, 0))],
        out_shape=[jax.ShapeDtypeStruct((m, d), BF16), jax.ShapeDtypeStruct((m, 1), F32)],
        scratch=(), vmem=6 * tm * d * 4 + 4 * MIB, name="norm_inputs", args=(x, g.reshape(1, d)))
    return hg, ssq


def _rmsnorm_kernel(x_ref, g_ref, o_ref):
    x = x_ref[...]
    y = x * lax.rsqrt(jnp.mean(x * x, axis=-1, keepdims=True) + EPS)
    o_ref[...] = (y * g_ref[...]).astype(o_ref.dtype)


def _rmsnorm(x, g):
    m, d = x.shape
    tm = _row_tile(m, ROW_TILE_NORM)
    (y,), _ = _call(
        _rmsnorm_kernel, grid=(m // tm,),
        in_specs=[pl.BlockSpec((tm, d), lambda i: (i, 0)), pl.BlockSpec((1, d), lambda i: (0, 0))],
        out_specs=[pl.BlockSpec((tm, d), lambda i: (i, 0))],
        out_shape=[jax.ShapeDtypeStruct((m, d), F32)],
        scratch=(), vmem=7 * tm * d * 4 + 4 * MIB, name="rmsnorm", args=(x, g.reshape(1, d)))
    return y


def _retention_kernel(logg_ref, hg_ref, ssq_ref, w_ref, cos_ref, sin_ref, s0_ref, *rest,
                      tm, hd, emit_state):
    if emit_state:
        a_ref, sout_ref, state_ref, q_s, k_s, v_s, g_s = rest
    else:
        a_ref, state_ref, q_s, k_s, v_s, g_s = rest
    rb = pl.program_id(1)
    h = pl.program_id(2)
    half = hd // 2

    @pl.when(rb == 0)
    def _():
        state_ref[h] = s0_ref[...]

    lg = logg_ref[h]
    ri = lax.broadcasted_iota(jnp.int32, (CHUNK, CHUNK), 0)
    ci = lax.broadcasted_iota(jnp.int32, (CHUNK, CHUNK), 1)
    rel = (ri - ci).astype(F32)
    inner_decay = jnp.where(rel >= 0, jnp.exp(lg * jnp.maximum(rel, 0.0)), 0.0)
    jj = lax.broadcasted_iota(jnp.int32, (CHUNK, 1), 0).astype(F32)
    q_decay = jnp.exp(lg * (jj + 1.0))
    k_decay = jnp.exp(lg * (CHUNK - 1.0 - jj))
    chunk_decay = jnp.exp(jnp.full((1, hd), lg * CHUNK, F32))

    def project(rows):
        proj = _dot(hg_ref[rows, :], w_ref[...]) * _inv_rms(ssq_ref[rows, :], hg_ref.shape[1])
        cos = cos_ref[rows, :]
        sin = sin_ref[rows, :]

        def rot(t):
            t1, t2 = t[:, :half], t[:, half:]
            return jnp.concatenate([t1 * cos - t2 * sin, t1 * sin + t2 * cos], axis=-1)

        q_s[rows, :] = rot(proj[:, 0 * hd:1 * hd]).astype(BF16)
        k_s[rows, :] = rot(proj[:, 1 * hd:2 * hd]) * (hd ** -0.5)
        v_s[rows, :] = proj[:, 2 * hd:3 * hd].astype(BF16)
        g = proj[:, 3 * hd:4 * hd]
        g_s[rows, :] = g * jax.nn.sigmoid(g)

    def chunk(c):
        rows = pl.ds(c * CHUNK, CHUNK)
        qc = q_s[rows, :]
        kc = k_s[rows, :]
        vc = v_s[rows, :]
        st = state_ref[h]
        scores = lax.dot_general(qc, kc.astype(BF16), (((1,), (1,)), ((), ())),
                                 preferred_element_type=F32) * inner_decay
        inner = _dot(scores.astype(BF16), vc)
        cross = _dot(qc, st.astype(BF16)) * q_decay
        state_ref[h] = st * chunk_decay + lax.dot_general(
            (kc * k_decay).astype(BF16), vc, (((0,), (0,)), ((), ())),
            preferred_element_type=F32)
        o = inner + cross
        o = o * lax.rsqrt(jnp.mean(o * o, axis=-1, keepdims=True) + EPS)
        a_ref[rows, :] = (o * g_s[rows, :]).astype(a_ref.dtype)

    gm = tm // ROW_GROUPS_RET if tm % (ROW_GROUPS_RET * CHUNK) == 0 else tm
    for r0 in range(0, tm, gm):
        project(pl.ds(r0, gm))
        for c in range(r0 // CHUNK, (r0 + gm) // CHUNK):
            chunk(c)

    if emit_state:
        sout_ref[...] = state_ref[h]


def _retention(hg, ssq, w_ret, cos, sin, state0, log_gamma, *, batch, emit_state, casts=()):
    m, d = hg.shape
    lb = m // batch
    heads = RET_HEADS
    hd = w_ret.shape[1] // (4 * heads)
    tm = _row_tile(lb, ROW_TILE_RET)
    nrb = lb // tm
    out_shape = [jax.ShapeDtypeStruct((m, heads * hd), BF16)]
    out_specs = [pl.BlockSpec((tm, hd), lambda b, r, h: (b * nrb + r, h))]
    if emit_state:
        out_shape.append(jax.ShapeDtypeStruct((heads, hd, hd), F32))
        out_specs.append(pl.BlockSpec((None, hd, hd), lambda b, r, h: (h, 0, 0)))
    vmem = (2 * tm * d * 2 + 2 * d * 4 * hd * 2 + 3 * tm * 4 * hd * 4 // ROW_GROUPS_RET
            + heads * hd * hd * 4 + 6 * MIB)
    return _call(
        functools.partial(_retention_kernel, tm=tm, hd=hd, emit_state=emit_state),
        grid=(batch, nrb, heads),
        in_specs=[
            pl.BlockSpec(memory_space=pltpu.SMEM),
            pl.BlockSpec((tm, d), lambda b, r, h: (b * nrb + r, 0)),
            pl.BlockSpec((tm, 1), lambda b, r, h: (b * nrb + r, 0)),
            pl.BlockSpec((d, 4 * hd), lambda b, r, h: (0, h)),
            pl.BlockSpec((tm, hd // 2), lambda b, r, h: (r, 0)),
            pl.BlockSpec((tm, hd // 2), lambda b, r, h: (r, 0)),
            pl.BlockSpec((None, hd, hd), lambda b, r, h: (h, 0, 0)),
        ],
        out_specs=out_specs, out_shape=out_shape,
        scratch=[
            pltpu.VMEM((heads, hd, hd), F32),
            pltpu.VMEM((tm, hd), BF16),
            pltpu.VMEM((tm, hd), F32),
            pltpu.VMEM((tm, hd), BF16),
            pltpu.VMEM((tm, hd), F32),
        ],
        vmem=vmem, name="retention",
        args=(log_gamma, hg, ssq, w_ret, cos, sin, state0), casts=casts)


def _conv_kernel(hg_ref, ssq_ref, w_ref, cw_ref, halo_ref, *rest, tm, tn, nrb, emit_tail):
    if emit_tail:
        b1_ref, tail_ref, carry_ref = rest
    else:
        b1_ref, carry_ref = rest
    i = pl.program_id(0)
    j = pl.program_id(1)

    @pl.when(i % nrb == 0)
    def _():
        carry_ref[j] = halo_ref[...]

    proj = _dot(hg_ref[...], w_ref[...]) * _inv_rms(ssq_ref[...], hg_ref.shape[1])
    zc = proj[:, :tn] * proj[:, 2 * tn:]
    prev = carry_ref[j]
    p1 = prev[SUBLANES - 1:SUBLANES, :]
    p2 = prev[SUBLANES - 2:SUBLANES - 1, :]
    row = lax.broadcasted_iota(jnp.int32, zc.shape, 0)
    z1 = jnp.where(row == 0, p1, pltpu.roll(zc, 1, 0))
    z2 = jnp.where(row == 0, p2, jnp.where(row == 1, p1, pltpu.roll(zc, 2, 0)))
    cw = cw_ref[...]
    conv = cw[0:1, :] * z2 + cw[1:2, :] * z1 + cw[2:3, :] * zc
    b1_ref[...] = (proj[:, tn:2 * tn] * conv).astype(b1_ref.dtype)
    tail = zc[tm - SUBLANES:, :]
    carry_ref[j] = tail
    if emit_tail:
        tail_ref[...] = tail


def _conv_mixer(hg, ssq, w_cbz, conv_w, halo, *, batch, emit_tail, casts=()):
    m, d = hg.shape
    lb = m // batch
    dc = conv_w.shape[1]
    tn = _conv_tile(dc)
    tm = _row_tile(lb, ROW_TILE_CONV)
    nrb = lb // tm
    ni, nj = m // tm, dc // tn
    out_shape = [jax.ShapeDtypeStruct((m, dc), BF16)]
    out_specs = [pl.BlockSpec((tm, tn), lambda i, j: (i, j))]
    if emit_tail:
        out_shape.append(jax.ShapeDtypeStruct((ni * SUBLANES, dc), F32))
        out_specs.append(pl.BlockSpec((SUBLANES, tn), lambda i, j: (i, j)))
    vmem = 2 * tm * d * 2 + 2 * d * 3 * tn * 2 + 3 * tm * 3 * tn * 4 + 4 * MIB
    return _call(
        functools.partial(_conv_kernel, tm=tm, tn=tn, nrb=nrb, emit_tail=emit_tail),
        grid=(ni, nj),
        in_specs=[
            pl.BlockSpec((tm, d), lambda i, j: (i, 0)),
            pl.BlockSpec((tm, 1), lambda i, j: (i, 0)),
            pl.BlockSpec((d, 3 * tn), lambda i, j: (0, j)),
            pl.BlockSpec((CONV_WIDTH, tn), lambda i, j: (0, j)),
            pl.BlockSpec((SUBLANES, tn), lambda i, j: (0, j)),
        ],
        out_specs=out_specs, out_shape=out_shape,
        scratch=[pltpu.VMEM((nj, SUBLANES, tn), F32)],
        vmem=vmem, name="conv_mixer", args=(hg, ssq, w_cbz, conv_w, halo), casts=casts)


def _merge_kernel(a_ref, b_ref, hg_ref, ssq_ref, wro_ref, wco_ref, wg_ref, o_ref, *, tn):
    ya = _dot(a_ref[...], wro_ref[...])
    yb = _dot(b_ref[...], wco_ref[...])
    gates = _dot(hg_ref[...], wg_ref[...]) * _inv_rms(ssq_ref[...], hg_ref.shape[1])
    o_ref[...] = (jax.nn.sigmoid(gates[:, :tn]) * ya
                  + jax.nn.sigmoid(gates[:, tn:]) * yb).astype(o_ref.dtype)


def _merge(a1, b1, hg, ssq, w_ro, w_co, w_gate, *, casts=()):
    m, d = hg.shape
    dr = a1.shape[1]
    dc = b1.shape[1]
    tn = _merge_tile(d)
    tm = _row_tile(m, ROW_TILE_MERGE)
    vmem = (2 * tm * (dr + dc + d) * 2 + 2 * (dr + dc + 2 * d) * tn * 2
            + 8 * tm * tn * 4 + 2 * MIB)
    (merged,), cast_out = _call(
        functools.partial(_merge_kernel, tn=tn),
        grid=(m // tm, d // tn),
        in_specs=[
            pl.BlockSpec((tm, dr), lambda i, j: (i, 0)),
            pl.BlockSpec((tm, dc), lambda i, j: (i, 0)),
            pl.BlockSpec((tm, d), lambda i, j: (i, 0)),
            pl.BlockSpec((tm, 1), lambda i, j: (i, 0)),
            pl.BlockSpec((dr, tn), lambda i, j: (0, j)),
            pl.BlockSpec((dc, tn), lambda i, j: (0, j)),
            pl.BlockSpec((d, 2 * tn), lambda i, j: (0, j)),
        ],
        out_specs=[pl.BlockSpec((tm, tn), lambda i, j: (i, j))],
        out_shape=[jax.ShapeDtypeStruct((m, d), BF16)],
        scratch=(), vmem=vmem, name="merge",
        args=(a1, b1, hg, ssq, w_ro, w_co, w_gate), casts=casts)
    return merged, cast_out


def _residual_kernel(a_ref, w_ref, h_ref, *rest, nt, emit_scaled):
    if emit_scaled:
        g_ref, o_ref, hg_ref, ssq_ref, acc_ref = rest
    else:
        (o_ref,) = rest
    j = pl.program_id(1)
    if emit_scaled:
        @pl.when(j == 0)
        def _():
            acc_ref[...] = jnp.zeros_like(acc_ref)

    hn = h_ref[...] + _dot(a_ref[...], w_ref[...])
    o_ref[...] = hn
    if emit_scaled:
        hg_ref[...] = (hn * g_ref[...]).astype(hg_ref.dtype)
        sq = hn * hn
        part = sq[:, :LANES]
        for c in range(1, sq.shape[1] // LANES):
            part = part + sq[:, c * LANES:(c + 1) * LANES]
        acc_ref[...] += part

        @pl.when(j == nt - 1)
        def _():
            ssq_ref[...] = jnp.sum(acc_ref[...], axis=-1, keepdims=True)


def _residual(a, w_tiles, h, g_next, *, k_splits, casts=()):
    m, kdim = a.shape
    nt, _, tn = w_tiles.shape
    d = nt * tn
    kb = kdim // k_splits
    tm = _row_tile(m, ROW_TILE_RES)
    casts = casts or ((),) * k_splits
    hg = ssq = None
    cast_out = []
    for s in range(k_splits):
        emit_scaled = g_next is not None and s == k_splits - 1
        in_specs = [
            pl.BlockSpec((tm, kb), lambda i, j, s=s: (i, s)),
            pl.BlockSpec((None, kb, tn), lambda i, j, s=s: (j, s, 0)),
            pl.BlockSpec((tm, tn), lambda i, j: (i, j)),
        ]
        out_specs = [pl.BlockSpec((tm, tn), lambda i, j: (i, j))]
        out_shape = [jax.ShapeDtypeStruct((m, d), F32)]
        args = [a, w_tiles, h]
        scratch = []
        vmem = 2 * tm * kb * 2 + 2 * kb * tn * 2 + 5 * tm * tn * 4 + 4 * MIB
        if emit_scaled:
            in_specs.append(pl.BlockSpec((1, tn), lambda i, j: (0, j)))
            args.append(g_next.reshape(1, d))
            out_specs += [pl.BlockSpec((tm, tn), lambda i, j: (i, j)),
                          pl.BlockSpec((tm, 1), lambda i, j: (i, 0))]
            out_shape += [jax.ShapeDtypeStruct((m, d), BF16), jax.ShapeDtypeStruct((m, 1), F32)]
            scratch.append(pltpu.VMEM((tm, LANES), F32))
            vmem += 2 * tm * tn * 2 + 4 * tm * LANES * 4
        outs, cw = _call(
            functools.partial(_residual_kernel, nt=nt, emit_scaled=emit_scaled),
            grid=(m // tm, nt), in_specs=in_specs, out_specs=out_specs, out_shape=out_shape,
            scratch=scratch, vmem=vmem, name="residual", args=args, casts=casts[s])
        cast_out += cw
        h = outs[0]
        if emit_scaled:
            hg, ssq = outs[1], outs[2]
    return (h, hg, ssq), cast_out


def _up_kernel(hg_ref, ssq_ref, w_ref, o_ref):
    t = jnp.maximum(_dot(hg_ref[...], w_ref[...]) * _inv_rms(ssq_ref[...], hg_ref.shape[1]), 0.0)
    o_ref[...] = (t * t).astype(o_ref.dtype)


def _up(hg, ssq, w_up, *, casts=()):
    m, d = hg.shape
    f = w_up.shape[1]
    tm = _row_tile(m, ROW_TILE_UP)
    tn = min(f, COL_TILE_UP)
    vmem = 2 * tm * d * 2 + 2 * d * tn * 2 + 2 * tm * tn * 2 + 3 * tm * tn * 4 + 2 * MIB
    (hid,), cast_out = _call(
        _up_kernel, grid=(m // tm, f // tn),
        in_specs=[pl.BlockSpec((tm, d), lambda i, j: (i, 0)),
                  pl.BlockSpec((tm, 1), lambda i, j: (i, 0)),
                  pl.BlockSpec((d, tn), lambda i, j: (0, j))],
        out_specs=[pl.BlockSpec((tm, tn), lambda i, j: (i, j))],
        out_shape=[jax.ShapeDtypeStruct((m, f), BF16)],
        scratch=(), vmem=vmem, name="mlp_up", args=(hg, ssq, w_up), casts=casts)
    return hid, cast_out


def _rope_tables(pos, hd):
    inv_freq = ROPE_BASE ** (-jnp.arange(0, hd, 2, dtype=F32) / hd)
    ang = pos[:, None] * inv_freq[None, :]
    return jnp.cos(ang), jnp.sin(ang)


def kernel(x, meta_tokens, norm_mix_g, w_in, conv_w, w_ret_out, w_conv_out, w_out,
           norm_mlp_g, w_up, w_down, final_norm_g):
    batch, seq, d = x.shape
    depth = w_in.shape[0]
    n_meta = meta_tokens.shape[0]
    dr = w_ret_out.shape[1]
    dc = w_conv_out.shape[1]
    f = w_up.shape[2]
    hd = dr // RET_HEADS
    assert seq % CHUNK == 0 and n_meta <= CHUNK
    meta_pad = CHUNK - n_meta

    def cast(layer, name):
        spec = {
            "w_ret": (w_in, 0, 4, dr, hd, False),
            "w_cbz": (w_in, 4 * dr, 3, dc, _conv_tile(dc), False),
            "w_gate": (w_in, 4 * dr + 3 * dc, 2, d, _merge_tile(d), False),
            "w_ro": (w_ret_out, 0, 1, d, _merge_tile(d), False),
            "w_co": (w_conv_out, 0, 1, d, _merge_tile(d), False),
            "w_o": (w_out, 0, 1, d, min(d, COL_TILE_RES_O), True),
            "w_up": (w_up, 0, 1, f, min(f, COL_TILE_UP), False),
            "w_down": (w_down, 0, 1, d, min(d, COL_TILE_RES_DOWN), True),
        }[name]
        return _Cast(spec[0], layer, *spec[1:])

    lw = [dict() for _ in range(depth)]
    lw[0]["w_ret"] = _cast_now(cast(0, "w_ret"))

    def riding(targets):
        targets = [(layer, name) for layer, name in targets if layer < depth]
        return targets, tuple(cast(layer, name) for layer, name in targets)

    def keep(targets, cast_out):
        for (layer, name), w in zip(targets, cast_out):
            lw[layer][name] = w

    log_gamma = jnp.asarray(
        np.log1p(-np.exp2(-5.0 - np.arange(RET_HEADS, dtype=np.float64))), F32)
    cos_x, sin_x = _rope_tables(jnp.arange(seq, dtype=F32) + n_meta, hd)
    cos_m, sin_m = _rope_tables(jnp.arange(CHUNK, dtype=F32) - meta_pad, hd)

    hm = jnp.concatenate([jnp.zeros((meta_pad, d), F32), meta_tokens.astype(F32)], axis=0)
    meta = dict(h=hm, batch=1, cos=cos_m, sin=sin_m, state=jnp.zeros((RET_HEADS, hd, hd), F32),
                halo=jnp.zeros((SUBLANES, dc), F32), is_meta=True)
    real = dict(h=x.reshape(batch * seq, d), batch=batch, cos=cos_x, sin=sin_x, is_meta=False)
    for st in (meta, real):
        st["hg"], st["ssq"] = _prep(st["h"], norm_mix_g[0])

    def stage(streams, fn, *target_groups):
        groups = [riding(t) for t in target_groups]
        for st in streams:
            casts = [() if st["is_meta"] else c for _, c in groups]
            updates, out = fn(st, *casts)
            st.update(updates)
            if not st["is_meta"]:
                keep([t for targets, _ in groups for t in targets], out)

    for i in range(depth):
        w = lw[i]
        nxt = i + 1
        streams = (meta, real)

        def ret(st, casts):
            outs, out = _retention(st["hg"], st["ssq"], w["w_ret"], st["cos"], st["sin"],
                                   st["state"], log_gamma, batch=st["batch"],
                                   emit_state=st["is_meta"], casts=casts)
            if st["is_meta"]:
                real["state"] = outs[1]
            return dict(a1=outs[0]), out

        def conv(st, casts):
            outs, out = _conv_mixer(st["hg"], st["ssq"], w["w_cbz"], conv_w[i], st["halo"],
                                    batch=st["batch"], emit_tail=st["is_meta"], casts=casts)
            if st["is_meta"]:
                real["halo"] = outs[1]
            return dict(b1=outs[0]), out

        def merge(st, casts):
            merged, out = _merge(st["a1"], st["b1"], st["hg"], st["ssq"], w["w_ro"], w["w_co"],
                                 w["w_gate"], casts=casts)
            return dict(merged=merged), out

        def project_out(st, casts):
            (h, hg, ssq), out = _residual(st["merged"], w["w_o"], st["h"], norm_mlp_g[i],
                                          k_splits=1, casts=(casts,))
            return dict(h=h, hg=hg, ssq=ssq), out

        def up(st, casts):
            hid, out = _up(st["hg"], st["ssq"], w["w_up"], casts=casts)
            return dict(hid=hid), out

        def down(st, *casts):
            g_next = norm_mix_g[nxt] if nxt < depth else None
            (h, hg, ssq), out = _residual(st["hid"], w["w_down"], st["h"], g_next,
                                          k_splits=K_SPLITS_DOWN, casts=casts)
            return dict(h=h, hg=hg, ssq=ssq), out

        def own(*names):
            return [(0, n) for n in names] if i == 0 else []

        stage(streams, ret, own("w_cbz", "w_up"))
        stage(streams, conv, own("w_gate", "w_ro", "w_co", "w_down") + [(nxt, "w_ro"), (nxt, "w_co")])
        if nxt == depth:
            streams = (real,)
        stage(streams, merge, own("w_o") + [(nxt, "w_gate")])
        stage(streams, project_out, [(nxt, "w_ret"), (nxt, "w_cbz")])
        stage(streams, up, [(nxt, "w_up")])
        down_targets = [[] for _ in range(K_SPLITS_DOWN)]
        down_targets[0].append((nxt, "w_down"))
        down_targets[-1].append((nxt, "w_o"))
        stage(streams, down, *down_targets)
    return _rmsnorm(real["h"], final_norm_g).reshape(batch, seq, d)
```

```python
import functools
from typing import NamedTuple

import numpy as np
import jax
import jax.numpy as jnp
from jax import lax
from jax.experimental import pallas as pl
from jax.experimental.pallas import tpu as pltpu

N_META = 16
RET_HEADS = 8
CONV_WIDTH = 3
CHUNK = 128
ROPE_BASE = 10000.0
EPS = 1e-6

V7X_VMEM_BYTES = 64 * 1024 * 1024
SUBLANES = 8
LANES = 128
MIB = 1024 * 1024

F32 = jnp.float32
BF16 = jnp.bfloat16

ROW_TILE_NORM = 256
ROW_TILE_RET = 1024
ROW_GROUPS_RET = 2
RET_CHUNK = 256
ROW_TILE_CONV = 1024
COL_TILE_CONV = 256
ROW_TILE_MERGE = 512
COL_TILE_MERGE = 512
ROW_TILE_RES = 1024
COL_TILE_RES_O = 512
COL_TILE_RES_DOWN = 256
K_SPLITS_DOWN = 2
ROW_TILE_UP = 1024
COL_TILE_UP = 1024
CAST_BLOCK_ELEMS = 1024 * 1024
MIN_CAST_BLOCK_ELEMS = 64 * 1024


def _dot(a, b):
    return jnp.dot(a, b, preferred_element_type=F32)


def _row_tile(m, want):
    return min(m, want)


def _conv_tile(dc):
    return min(dc, COL_TILE_CONV)


def _merge_tile(d):
    return min(d, COL_TILE_MERGE)


def _inv_rms(ssq, d):
    return lax.rsqrt(ssq / d + EPS)


class _Cast(NamedTuple):
    w: jax.Array
    layer: int
    col0: int
    sections: int
    width: int
    tile: int
    tile_major: bool
    block_elems: int = 0


def _cast_geometry(c):
    kdim = c.w.shape[1]
    tiles_per_section = c.width // c.tile
    n_tiles = c.sections * tiles_per_section
    tk = min(kdim, c.block_elems // c.tile)
    return kdim, tiles_per_section, n_tiles, tk, kdim // tk


def _fit_blocks(c, n_steps):
    c = c._replace(block_elems=MIN_CAST_BLOCK_ELEMS)
    while _cast_geometry(c)[2] * _cast_geometry(c)[4] > n_steps:
        assert _cast_geometry(c)[3] < c.w.shape[1], (c.block_elems, n_steps)
        c = c._replace(block_elems=2 * c.block_elems)
    return c


def _cast_specs(c, step_of):
    kdim, tiles_per_section, n_tiles, tk, kblocks = _cast_geometry(c)
    n_blocks = n_tiles * kblocks
    base = c.col0 // c.tile

    def tile_and_kblock(*idx):
        t = jnp.minimum(step_of(*idx), n_blocks - 1)
        return t // kblocks, t % kblocks

    def src(*idx):
        p, kk = tile_and_kblock(*idx)
        return c.layer, kk, base + (p % c.sections) * tiles_per_section + p // c.sections

    if c.tile_major:
        shape = (n_tiles, kdim, c.tile)
        block = (None, tk, c.tile)

        def dst(*idx):
            p, kk = tile_and_kblock(*idx)
            return p, kk, 0
    else:
        shape = (kdim, n_tiles * c.tile)
        block = (tk, c.tile)

        def dst(*idx):
            p, kk = tile_and_kblock(*idx)
            return kk, p

    return (pl.BlockSpec((None, tk, c.tile), src), pl.BlockSpec(block, dst),
            jax.ShapeDtypeStruct(shape, BF16), n_blocks)


def _call(body, *, grid, in_specs, out_specs, out_shape, scratch, vmem, name, args, casts=()):
    n_in, n_out, n_cast = len(in_specs), len(out_specs), len(casts)

    def step_of(*idx):
        t = idx[0]
        for size, i in zip(grid[1:], idx[1:]):
            t = t * size + i
        return t

    n_steps = int(np.prod(grid))
    casts = [c if c.block_elems else _fit_blocks(c, n_steps) for c in casts]
    specs = [_cast_specs(c, step_of) for c in casts]
    for c, s in zip(casts, specs):
        assert s[3] <= n_steps, (name, s[3], n_steps)
        vmem += 2 * c.block_elems * (4 + 2)

    def kernel(*refs):
        ins = refs[:n_in]
        cast_in = refs[n_in:n_in + n_cast]
        outs = refs[n_in + n_cast:n_in + n_cast + n_out]
        cast_out = refs[n_in + n_cast + n_out:n_in + 2 * n_cast + n_out]
        for x_ref, o_ref in zip(cast_in, cast_out):
            o_ref[...] = x_ref[...].astype(o_ref.dtype)
        body(*ins, *outs, *refs[n_in + 2 * n_cast + n_out:])

    assert vmem <= V7X_VMEM_BYTES - 4 * MIB, (name, vmem)
    res = pl.pallas_call(
        kernel,
        grid=grid,
        in_specs=[*in_specs, *[s[0] for s in specs]],
        out_specs=[*out_specs, *[s[1] for s in specs]],
        out_shape=[*out_shape, *[s[2] for s in specs]],
        scratch_shapes=list(scratch),
        compiler_params=pltpu.CompilerParams(
            dimension_semantics=("arbitrary",) * len(grid), vmem_limit_bytes=int(vmem)),
        name=name,
    )(*args, *[c.w for c in casts])
    return list(res[:n_out]), list(res[n_out:])


def _cast_now(c):
    c = c._replace(block_elems=CAST_BLOCK_ELEMS)
    n_tiles, kblocks = _cast_geometry(c)[2], _cast_geometry(c)[4]
    _, (w,) = _call(lambda: None, grid=(n_tiles * kblocks,), in_specs=[], out_specs=[],
                    out_shape=[], scratch=(), vmem=4 * MIB, name="cast_weight", args=(),
                    casts=(c,))
    return w


def _prep_kernel(x_ref, g_ref, hg_ref, ssq_ref):
    x = x_ref[...]
    hg_ref[...] = (x * g_ref[...]).astype(hg_ref.dtype)
    ssq_ref[...] = jnp.sum(x * x, axis=-1, keepdims=True)


def _prep(x, g):
    m, d = x.shape
    tm = _row_tile(m, ROW_TILE_NORM)
    (hg, ssq), _ = _call(
        _prep_kernel, grid=(m // tm,),
        in_specs=[pl.BlockSpec((tm, d), lambda i: (i, 0)), pl.BlockSpec((1, d), lambda i: (0, 0))],
        out_specs=[pl.BlockSpec((tm, d), lambda i: (i, 0)), pl.BlockSpec((tm, 1), lambda i: (i, 0))],
        out_shape=[jax.ShapeDtypeStruct((m, d), BF16), jax.ShapeDtypeStruct((m, 1), F32)],
        scratch=(), vmem=6 * tm * d * 4 + 4 * MIB, name="norm_inputs", args=(x, g.reshape(1, d)))
    return hg, ssq


def _rmsnorm_kernel(x_ref, g_ref, o_ref):
    x = x_ref[...]
    y = x * lax.rsqrt(jnp.mean(x * x, axis=-1, keepdims=True) + EPS)
    o_ref[...] = (y * g_ref[...]).astype(o_ref.dtype)


def _rmsnorm(x, g):
    m, d = x.shape
    tm = _row_tile(m, ROW_TILE_NORM)
    (y,), _ = _call(
        _rmsnorm_kernel, grid=(m // tm,),
        in_specs=[pl.BlockSpec((tm, d), lambda i: (i, 0)), pl.BlockSpec((1, d), lambda i: (0, 0))],
        out_specs=[pl.BlockSpec((tm, d), lambda i: (i, 0))],
        out_shape=[jax.ShapeDtypeStruct((m, d), F32)],
        scratch=(), vmem=7 * tm * d * 4 + 4 * MIB, name="rmsnorm", args=(x, g.reshape(1, d)))
    return y


def _retention_kernel(logg_ref, hg_ref, ssq_ref, w_ref, cos_ref, sin_ref, s0_ref, *rest,
                      tm, hd, chunk, emit_state):
    if emit_state:
        a_ref, sout_ref, state_ref, q_s, k_s, v_s, g_s = rest
    else:
        a_ref, state_ref, q_s, k_s, v_s, g_s = rest
    rb = pl.program_id(1)
    h = pl.program_id(2)
    half = hd // 2

    @pl.when(rb == 0)
    def _():
        state_ref[h] = s0_ref[...]

    lg = logg_ref[h]
    ri = lax.broadcasted_iota(jnp.int32, (chunk, chunk), 0)
    ci = lax.broadcasted_iota(jnp.int32, (chunk, chunk), 1)
    rel = (ri - ci).astype(F32)
    inner_decay = jnp.where(rel >= 0, jnp.exp(lg * jnp.maximum(rel, 0.0)), 0.0)
    jj = lax.broadcasted_iota(jnp.int32, (chunk, 1), 0).astype(F32)
    q_decay = jnp.exp(lg * (jj + 1.0))
    k_decay = jnp.exp(lg * (chunk - 1.0 - jj))
    chunk_decay = jnp.exp(jnp.full((1, hd), lg * chunk, F32))

    def project(rows):
        proj = _dot(hg_ref[rows, :], w_ref[...]) * _inv_rms(ssq_ref[rows, :], hg_ref.shape[1])
        cos = cos_ref[rows, :]
        sin = sin_ref[rows, :]

        def rot(t):
            t1, t2 = t[:, :half], t[:, half:]
            return jnp.concatenate([t1 * cos - t2 * sin, t1 * sin + t2 * cos], axis=-1)

        q_s[rows, :] = rot(proj[:, 0 * hd:1 * hd]).astype(BF16)
        k_s[rows, :] = rot(proj[:, 1 * hd:2 * hd]) * (hd ** -0.5)
        v_s[rows, :] = proj[:, 2 * hd:3 * hd].astype(BF16)
        g = proj[:, 3 * hd:4 * hd]
        g_s[rows, :] = g * jax.nn.sigmoid(g)

    def recur(c):
        rows = pl.ds(c * chunk, chunk)
        qc = q_s[rows, :]
        kc = k_s[rows, :]
        vc = v_s[rows, :]
        st = state_ref[h]
        scores = lax.dot_general(qc, kc.astype(BF16), (((1,), (1,)), ((), ())),
                                 preferred_element_type=F32) * inner_decay
        inner = _dot(scores.astype(BF16), vc)
        cross = _dot(qc, st.astype(BF16)) * q_decay
        state_ref[h] = st * chunk_decay + lax.dot_general(
            (kc * k_decay).astype(BF16), vc, (((0,), (0,)), ((), ())),
            preferred_element_type=F32)
        o = inner + cross
        o = o * lax.rsqrt(jnp.mean(o * o, axis=-1, keepdims=True) + EPS)
        a_ref[rows, :] = (o * g_s[rows, :]).astype(a_ref.dtype)

    gm = tm // ROW_GROUPS_RET if tm % (ROW_GROUPS_RET * chunk) == 0 else tm
    for r0 in range(0, tm, gm):
        project(pl.ds(r0, gm))
        for c in range(r0 // chunk, (r0 + gm) // chunk):
            recur(c)

    if emit_state:
        sout_ref[...] = state_ref[h]


def _retention(hg, ssq, w_ret, cos, sin, state0, log_gamma, *, batch, emit_state, casts=()):
    m, d = hg.shape
    lb = m // batch
    heads = RET_HEADS
    hd = w_ret.shape[1] // (4 * heads)
    tm = _row_tile(lb, ROW_TILE_RET)
    nrb = lb // tm
    chunk = RET_CHUNK if tm % RET_CHUNK == 0 else CHUNK
    out_shape = [jax.ShapeDtypeStruct((m, heads * hd), BF16)]
    out_specs = [pl.BlockSpec((tm, hd), lambda b, r, h: (b * nrb + r, h))]
    if emit_state:
        out_shape.append(jax.ShapeDtypeStruct((heads, hd, hd), F32))
        out_specs.append(pl.BlockSpec((None, hd, hd), lambda b, r, h: (h, 0, 0)))
    vmem = (2 * tm * d * 2 + 2 * d * 4 * hd * 2 + 3 * tm * 4 * hd * 4 // ROW_GROUPS_RET
            + heads * hd * hd * 4 + 6 * MIB)
    return _call(
        functools.partial(_retention_kernel, tm=tm, hd=hd, chunk=chunk, emit_state=emit_state),
        grid=(batch, nrb, heads),
        in_specs=[
            pl.BlockSpec(memory_space=pltpu.SMEM),
            pl.BlockSpec((tm, d), lambda b, r, h: (b * nrb + r, 0)),
            pl.BlockSpec((tm, 1), lambda b, r, h: (b * nrb + r, 0)),
            pl.BlockSpec((d, 4 * hd), lambda b, r, h: (0, h)),
            pl.BlockSpec((tm, hd // 2), lambda b, r, h: (r, 0)),
            pl.BlockSpec((tm, hd // 2), lambda b, r, h: (r, 0)),
            pl.BlockSpec((None, hd, hd), lambda b, r, h: (h, 0, 0)),
        ],
        out_specs=out_specs, out_shape=out_shape,
        scratch=[
            pltpu.VMEM((heads, hd, hd), F32),
            pltpu.VMEM((tm, hd), BF16),
            pltpu.VMEM((tm, hd), F32),
            pltpu.VMEM((tm, hd), BF16),
            pltpu.VMEM((tm, hd), F32),
        ],
        vmem=vmem, name="retention",
        args=(log_gamma, hg, ssq, w_ret, cos, sin, state0), casts=casts)


def _conv_kernel(hg_ref, ssq_ref, w_ref, cw_ref, halo_ref, *rest, tm, tn, nrb, emit_tail):
    if emit_tail:
        b1_ref, tail_ref, carry_ref = rest
    else:
        b1_ref, carry_ref = rest
    i = pl.program_id(0)
    j = pl.program_id(1)

    @pl.when(i % nrb == 0)
    def _():
        carry_ref[j] = halo_ref[...]

    proj = _dot(hg_ref[...], w_ref[...]) * _inv_rms(ssq_ref[...], hg_ref.shape[1])
    zc = proj[:, :tn] * proj[:, 2 * tn:]
    prev = carry_ref[j]
    p1 = prev[SUBLANES - 1:SUBLANES, :]
    p2 = prev[SUBLANES - 2:SUBLANES - 1, :]
    row = lax.broadcasted_iota(jnp.int32, zc.shape, 0)
    z1 = jnp.where(row == 0, p1, pltpu.roll(zc, 1, 0))
    z2 = jnp.where(row == 0, p2, jnp.where(row == 1, p1, pltpu.roll(zc, 2, 0)))
    cw = cw_ref[...]
    conv = cw[0:1, :] * z2 + cw[1:2, :] * z1 + cw[2:3, :] * zc
    b1_ref[...] = (proj[:, tn:2 * tn] * conv).astype(b1_ref.dtype)
    tail = zc[tm - SUBLANES:, :]
    carry_ref[j] = tail
    if emit_tail:
        tail_ref[...] = tail


def _conv_mixer(hg, ssq, w_cbz, conv_w, halo, *, batch, emit_tail, casts=()):
    m, d = hg.shape
    lb = m // batch
    dc = conv_w.shape[1]
    tn = _conv_tile(dc)
    tm = _row_tile(lb, ROW_TILE_CONV)
    nrb = lb // tm
    ni, nj = m // tm, dc // tn
    out_shape = [jax.ShapeDtypeStruct((m, dc), BF16)]
    out_specs = [pl.BlockSpec((tm, tn), lambda i, j: (i, j))]
    if emit_tail:
        out_shape.append(jax.ShapeDtypeStruct((ni * SUBLANES, dc), F32))
        out_specs.append(pl.BlockSpec((SUBLANES, tn), lambda i, j: (i, j)))
    vmem = 2 * tm * d * 2 + 2 * d * 3 * tn * 2 + 3 * tm * 3 * tn * 4 + 4 * MIB
    return _call(
        functools.partial(_conv_kernel, tm=tm, tn=tn, nrb=nrb, emit_tail=emit_tail),
        grid=(ni, nj),
        in_specs=[
            pl.BlockSpec((tm, d), lambda i, j: (i, 0)),
            pl.BlockSpec((tm, 1), lambda i, j: (i, 0)),
            pl.BlockSpec((d, 3 * tn), lambda i, j: (0, j)),
            pl.BlockSpec((CONV_WIDTH, tn), lambda i, j: (0, j)),
            pl.BlockSpec((SUBLANES, tn), lambda i, j: (0, j)),
        ],
        out_specs=out_specs, out_shape=out_shape,
        scratch=[pltpu.VMEM((nj, SUBLANES, tn), F32)],
        vmem=vmem, name="conv_mixer", args=(hg, ssq, w_cbz, conv_w, halo), casts=casts)


def _merge_kernel(a_ref, b_ref, hg_ref, ssq_ref, wro_ref, wco_ref, wg_ref, o_ref, *, tn):
    ya = _dot(a_ref[...], wro_ref[...])
    yb = _dot(b_ref[...], wco_ref[...])
    gates = _dot(hg_ref[...], wg_ref[...]) * _inv_rms(ssq_ref[...], hg_ref.shape[1])
    o_ref[...] = (jax.nn.sigmoid(gates[:, :tn]) * ya
                  + jax.nn.sigmoid(gates[:, tn:]) * yb).astype(o_ref.dtype)


def _merge(a1, b1, hg, ssq, w_ro, w_co, w_gate, *, casts=()):
    m, d = hg.shape
    dr = a1.shape[1]
    dc = b1.shape[1]
    tn = _merge_tile(d)
    tm = _row_tile(m, ROW_TILE_MERGE)
    vmem = (2 * tm * (dr + dc + d) * 2 + 2 * (dr + dc + 2 * d) * tn * 2
            + 8 * tm * tn * 4 + 2 * MIB)
    (merged,), cast_out = _call(
        functools.partial(_merge_kernel, tn=tn),
        grid=(m // tm, d // tn),
        in_specs=[
            pl.BlockSpec((tm, dr), lambda i, j: (i, 0)),
            pl.BlockSpec((tm, dc), lambda i, j: (i, 0)),
            pl.BlockSpec((tm, d), lambda i, j: (i, 0)),
            pl.BlockSpec((tm, 1), lambda i, j: (i, 0)),
            pl.BlockSpec((dr, tn), lambda i, j: (0, j)),
            pl.BlockSpec((dc, tn), lambda i, j: (0, j)),
            pl.BlockSpec((d, 2 * tn), lambda i, j: (0, j)),
        ],
        out_specs=[pl.BlockSpec((tm, tn), lambda i, j: (i, j))],
        out_shape=[jax.ShapeDtypeStruct((m, d), BF16)],
        scratch=(), vmem=vmem, name="merge",
        args=(a1, b1, hg, ssq, w_ro, w_co, w_gate), casts=casts)
    return merged, cast_out


def _residual_kernel(a_ref, w_ref, h_ref, *rest, nt, emit_scaled):
    if emit_scaled:
        g_ref, o_ref, hg_ref, ssq_ref, acc_ref = rest
    else:
        (o_ref,) = rest
    j = pl.program_id(1)
    if emit_scaled:
        @pl.when(j == 0)
        def _():
            acc_ref[...] = jnp.zeros_like(acc_ref)

    hn = h_ref[...] + _dot(a_ref[...], w_ref[...])
    o_ref[...] = hn
    if emit_scaled:
        hg_ref[...] = (hn * g_ref[...]).astype(hg_ref.dtype)
        sq = hn * hn
        part = sq[:, :LANES]
        for c in range(1, sq.shape[1] // LANES):
            part = part + sq[:, c * LANES:(c + 1) * LANES]
        acc_ref[...] += part

        @pl.when(j == nt - 1)
        def _():
            ssq_ref[...] = jnp.sum(acc_ref[...], axis=-1, keepdims=True)


def _residual(a, w_tiles, h, g_next, *, k_splits, casts=()):
    m, kdim = a.shape
    nt, _, tn = w_tiles.shape
    d = nt * tn
    kb = kdim // k_splits
    tm = _row_tile(m, ROW_TILE_RES)
    casts = casts or ((),) * k_splits
    hg = ssq = None
    cast_out = []
    for s in range(k_splits):
        emit_scaled = g_next is not None and s == k_splits - 1
        in_specs = [
            pl.BlockSpec((tm, kb), lambda i, j, s=s: (i, s)),
            pl.BlockSpec((None, kb, tn), lambda i, j, s=s: (j, s, 0)),
            pl.BlockSpec((tm, tn), lambda i, j: (i, j)),
        ]
        out_specs = [pl.BlockSpec((tm, tn), lambda i, j: (i, j))]
        out_shape = [jax.ShapeDtypeStruct((m, d), F32)]
        args = [a, w_tiles, h]
        scratch = []
        vmem = 2 * tm * kb * 2 + 2 * kb * tn * 2 + 5 * tm * tn * 4 + 4 * MIB
        if emit_scaled:
            in_specs.append(pl.BlockSpec((1, tn), lambda i, j: (0, j)))
            args.append(g_next.reshape(1, d))
            out_specs += [pl.BlockSpec((tm, tn), lambda i, j: (i, j)),
                          pl.BlockSpec((tm, 1), lambda i, j: (i, 0))]
            out_shape += [jax.ShapeDtypeStruct((m, d), BF16), jax.ShapeDtypeStruct((m, 1), F32)]
            scratch.append(pltpu.VMEM((tm, LANES), F32))
            vmem += 2 * tm * tn * 2 + 4 * tm * LANES * 4
        outs, cw = _call(
            functools.partial(_residual_kernel, nt=nt, emit_scaled=emit_scaled),
            grid=(m // tm, nt), in_specs=in_specs, out_specs=out_specs, out_shape=out_shape,
            scratch=scratch, vmem=vmem, name="residual", args=args, casts=casts[s])
        cast_out += cw
        h = outs[0]
        if emit_scaled:
            hg, ssq = outs[1], outs[2]
    return (h, hg, ssq), cast_out


def _up_kernel(hg_ref, ssq_ref, w_ref, o_ref):
    t = jnp.maximum(_dot(hg_ref[...], w_ref[...]) * _inv_rms(ssq_ref[...], hg_ref.shape[1]), 0.0)
    o_ref[...] = (t * t).astype(o_ref.dtype)


def _up(hg, ssq, w_up, *, casts=()):
    m, d = hg.shape
    f = w_up.shape[1]
    tm = _row_tile(m, ROW_TILE_UP)
    tn = min(f, COL_TILE_UP)
    vmem = 2 * tm * d * 2 + 2 * d * tn * 2 + 2 * tm * tn * 2 + 3 * tm * tn * 4 + 2 * MIB
    (hid,), cast_out = _call(
        _up_kernel, grid=(m // tm, f // tn),
        in_specs=[pl.BlockSpec((tm, d), lambda i, j: (i, 0)),
                  pl.BlockSpec((tm, 1), lambda i, j: (i, 0)),
                  pl.BlockSpec((d, tn), lambda i, j: (0, j))],
        out_specs=[pl.BlockSpec((tm, tn), lambda i, j: (i, j))],
        out_shape=[jax.ShapeDtypeStruct((m, f), BF16)],
        scratch=(), vmem=vmem, name="mlp_up", args=(hg, ssq, w_up), casts=casts)
    return hid, cast_out


def _rope_tables(pos, hd):
    inv_freq = ROPE_BASE ** (-jnp.arange(0, hd, 2, dtype=F32) / hd)
    ang = pos[:, None] * inv_freq[None, :]
    return jnp.cos(ang), jnp.sin(ang)


def kernel(x, meta_tokens, norm_mix_g, w_in, conv_w, w_ret_out, w_conv_out, w_out,
           norm_mlp_g, w_up, w_down, final_norm_g):
    batch, seq, d = x.shape
    depth = w_in.shape[0]
    n_meta = meta_tokens.shape[0]
    dr = w_ret_out.shape[1]
    dc = w_conv_out.shape[1]
    f = w_up.shape[2]
    hd = dr // RET_HEADS
    assert seq % CHUNK == 0 and n_meta <= CHUNK
    meta_pad = CHUNK - n_meta

    def cast(layer, name):
        spec = {
            "w_ret": (w_in, 0, 4, dr, hd, False),
            "w_cbz": (w_in, 4 * dr, 3, dc, _conv_tile(dc), False),
            "w_gate": (w_in, 4 * dr + 3 * dc, 2, d, _merge_tile(d), False),
            "w_ro": (w_ret_out, 0, 1, d, _merge_tile(d), False),
            "w_co": (w_conv_out, 0, 1, d, _merge_tile(d), False),
            "w_o": (w_out, 0, 1, d, min(d, COL_TILE_RES_O), True),
            "w_up": (w_up, 0, 1, f, min(f, COL_TILE_UP), False),
            "w_down": (w_down, 0, 1, d, min(d, COL_TILE_RES_DOWN), True),
        }[name]
        return _Cast(spec[0], layer, *spec[1:])

    lw = [dict() for _ in range(depth)]
    lw[0]["w_ret"] = _cast_now(cast(0, "w_ret"))

    def riding(targets):
        targets = [(layer, name) for layer, name in targets if layer < depth]
        return targets, tuple(cast(layer, name) for layer, name in targets)

    def keep(targets, cast_out):
        for (layer, name), w in zip(targets, cast_out):
            lw[layer][name] = w

    log_gamma = jnp.asarray(
        np.log1p(-np.exp2(-5.0 - np.arange(RET_HEADS, dtype=np.float64))), F32)
    cos_x, sin_x = _rope_tables(jnp.arange(seq, dtype=F32) + n_meta, hd)
    cos_m, sin_m = _rope_tables(jnp.arange(CHUNK, dtype=F32) - meta_pad, hd)

    hm = jnp.concatenate([jnp.zeros((meta_pad, d), F32), meta_tokens.astype(F32)], axis=0)
    meta = dict(h=hm, batch=1, cos=cos_m, sin=sin_m, state=jnp.zeros((RET_HEADS, hd, hd), F32),
                halo=jnp.zeros((SUBLANES, dc), F32), is_meta=True)
    real = dict(h=x.reshape(batch * seq, d), batch=batch, cos=cos_x, sin=sin_x, is_meta=False)
    for st in (meta, real):
        st["hg"], st["ssq"] = _prep(st["h"], norm_mix_g[0])

    def stage(streams, fn, *target_groups):
        groups = [riding(t) for t in target_groups]
        for st in streams:
            casts = [() if st["is_meta"] else c for _, c in groups]
            updates, out = fn(st, *casts)
            st.update(updates)
            if not st["is_meta"]:
                keep([t for targets, _ in groups for t in targets], out)

    for i in range(depth):
        w = lw[i]
        nxt = i + 1
        streams = (meta, real)

        def ret(st, casts):
            outs, out = _retention(st["hg"], st["ssq"], w["w_ret"], st["cos"], st["sin"],
                                   st["state"], log_gamma, batch=st["batch"],
                                   emit_state=st["is_meta"], casts=casts)
            if st["is_meta"]:
                real["state"] = outs[1]
            return dict(a1=outs[0]), out

        def conv(st, casts):
            outs, out = _conv_mixer(st["hg"], st["ssq"], w["w_cbz"], conv_w[i], st["halo"],
                                    batch=st["batch"], emit_tail=st["is_meta"], casts=casts)
            if st["is_meta"]:
                real["halo"] = outs[1]
            return dict(b1=outs[0]), out

        def merge(st, casts):
            merged, out = _merge(st["a1"], st["b1"], st["hg"], st["ssq"], w["w_ro"], w["w_co"],
                                 w["w_gate"], casts=casts)
            return dict(merged=merged), out

        def project_out(st, casts):
            (h, hg, ssq), out = _residual(st["merged"], w["w_o"], st["h"], norm_mlp_g[i],
                                          k_splits=1, casts=(casts,))
            return dict(h=h, hg=hg, ssq=ssq), out

        def up(st, casts):
            hid, out = _up(st["hg"], st["ssq"], w["w_up"], casts=casts)
            return dict(hid=hid), out

        def down(st, *casts):
            g_next = norm_mix_g[nxt] if nxt < depth else None
            (h, hg, ssq), out = _residual(st["hid"], w["w_down"], st["h"], g_next,
                                          k_splits=K_SPLITS_DOWN, casts=casts)
            return dict(h=h, hg=hg, ssq=ssq), out

        def own(*names):
            return [(0, n) for n in names] if i == 0 else []

        stage(streams, ret, own("w_cbz", "w_up"))
        stage(streams, conv, own("w_gate", "w_ro", "w_co", "w_down") + [(nxt, "w_ro"), (nxt, "w_co")])
        if nxt == depth:
            streams = (real,)
        stage(streams, merge, own("w_o") + [(nxt, "w_gate")])
        stage(streams, project_out, [(nxt, "w_ret"), (nxt, "w_cbz")])
        stage(streams, up, [(nxt, "w_up")])
        down_targets = [[] for _ in range(K_SPLITS_DOWN)]
        down_targets[0].append((nxt, "w_down"))
        down_targets[-1].append((nxt, "w_o"))
        stage(streams, down, *down_targets)
    return _rmsnorm(real["h"], final_norm_g).reshape(batch, seq, d)
```

```python
import functools
from typing import NamedTuple

import numpy as np
import jax
import jax.numpy as jnp
from jax import lax
from jax.experimental import pallas as pl
from jax.experimental.pallas import tpu as pltpu

RET_HEADS = 8
CONV_WIDTH = 3
CHUNK = 128
ROPE_BASE = 10000.0
EPS = 1e-6

MIB = 1024 * 1024
V7X_VMEM_BYTES = 64 * MIB
VMEM_CAP_BYTES = V7X_VMEM_BYTES - 4 * MIB
SUBLANES = 8
LANES = 128

F32 = jnp.float32
BF16 = jnp.bfloat16

ROW_TILE_NORM = 256
ROW_TILE_RET = 1024
ROW_GROUPS_RET = 2
RET_CHUNK = 256
ROW_TILE_CONV = 1024
COL_TILE_CONV = 256
ROW_TILE_MERGE = 512
COL_TILE_MERGE = 512
ROW_TILE_RES = 1024
COL_TILE_RES_O = 512
COL_TILE_RES_DOWN = 256
K_SPLITS_DOWN = 2
ROW_TILE_UP = 1024
COL_TILE_UP = 1024
CAST_BLOCK_ELEMS = 1024 * 1024
MIN_CAST_BLOCK_ELEMS = 64 * 1024


def _dot(a, b):
    return jnp.dot(a, b, preferred_element_type=F32)


def _row_tile(m, want):
    tm = min(m, want)
    assert m % tm == 0, (m, tm)
    return tm


def _conv_tile(dc):
    return min(dc, COL_TILE_CONV)


def _merge_tile(d):
    return min(d, COL_TILE_MERGE)


def _inv_rms(ssq, d):
    return lax.rsqrt(ssq / d + EPS)


class _Cast(NamedTuple):
    w: jax.Array
    layer: int
    col0: int
    sections: int
    width: int
    tile: int
    tile_major: bool
    block_elems: int = 0


def _cast_geometry(c):
    kdim = c.w.shape[1]
    tiles_per_section = c.width // c.tile
    n_tiles = c.sections * tiles_per_section
    tk = min(kdim, c.block_elems // c.tile)
    return kdim, tiles_per_section, n_tiles, tk, kdim // tk


def _fit_blocks(c, n_steps):
    c = c._replace(block_elems=MIN_CAST_BLOCK_ELEMS)
    while _cast_geometry(c)[2] * _cast_geometry(c)[4] > n_steps:
        assert _cast_geometry(c)[3] < c.w.shape[1], (c.block_elems, n_steps)
        c = c._replace(block_elems=2 * c.block_elems)
    return c


def _cast_specs(c, step_of):
    kdim, tiles_per_section, n_tiles, tk, kblocks = _cast_geometry(c)
    n_blocks = n_tiles * kblocks
    base = c.col0 // c.tile

    def tile_and_kblock(*idx):
        t = jnp.minimum(step_of(*idx), n_blocks - 1)
        return t // kblocks, t % kblocks

    def src(*idx):
        p, kk = tile_and_kblock(*idx)
        return c.layer, kk, base + (p % c.sections) * tiles_per_section + p // c.sections

    if c.tile_major:
        shape = (n_tiles, kdim, c.tile)
        block = (None, tk, c.tile)

        def dst(*idx):
            p, kk = tile_and_kblock(*idx)
            return p, kk, 0
    else:
        shape = (kdim, n_tiles * c.tile)
        block = (tk, c.tile)

        def dst(*idx):
            p, kk = tile_and_kblock(*idx)
            return kk, p

    return (pl.BlockSpec((None, tk, c.tile), src), pl.BlockSpec(block, dst),
            jax.ShapeDtypeStruct(shape, BF16), n_blocks)


def _call(body, *, grid, in_specs, out_specs, out_shape, scratch, vmem, name, args, casts=()):
    n_in, n_out, n_cast = len(in_specs), len(out_specs), len(casts)

    def step_of(*idx):
        t = idx[0]
        for size, i in zip(grid[1:], idx[1:]):
            t = t * size + i
        return t

    n_steps = int(np.prod(grid))
    casts = [c if c.block_elems else _fit_blocks(c, n_steps) for c in casts]
    specs = [_cast_specs(c, step_of) for c in casts]
    for c, s in zip(casts, specs):
        assert s[3] <= n_steps, (name, s[3], n_steps)
        vmem += 2 * c.block_elems * (4 + 2)

    def kernel(*refs):
        ins = refs[:n_in]
        cast_in = refs[n_in:n_in + n_cast]
        outs = refs[n_in + n_cast:n_in + n_cast + n_out]
        cast_out = refs[n_in + n_cast + n_out:n_in + 2 * n_cast + n_out]
        for x_ref, o_ref in zip(cast_in, cast_out):
            o_ref[...] = x_ref[...].astype(o_ref.dtype)
        body(*ins, *outs, *refs[n_in + 2 * n_cast + n_out:])

    assert vmem <= VMEM_CAP_BYTES, (name, vmem)
    res = pl.pallas_call(
        kernel,
        grid=grid,
        in_specs=[*in_specs, *[s[0] for s in specs]],
        out_specs=[*out_specs, *[s[1] for s in specs]],
        out_shape=[*out_shape, *[s[2] for s in specs]],
        scratch_shapes=list(scratch),
        compiler_params=pltpu.CompilerParams(
            dimension_semantics=("arbitrary",) * len(grid), vmem_limit_bytes=int(vmem)),
        name=name,
    )(*args, *[c.w for c in casts])
    return list(res[:n_out]), list(res[n_out:])


def _cast_now(c):
    c = c._replace(block_elems=CAST_BLOCK_ELEMS)
    n_tiles, kblocks = _cast_geometry(c)[2], _cast_geometry(c)[4]
    _, (w,) = _call(lambda: None, grid=(n_tiles * kblocks,), in_specs=[], out_specs=[],
                    out_shape=[], scratch=(), vmem=4 * MIB, name="cast_weight", args=(),
                    casts=(c,))
    return w


def _prep_kernel(x_ref, g_ref, hg_ref, ssq_ref):
    x = x_ref[...]
    hg_ref[...] = (x * g_ref[...]).astype(hg_ref.dtype)
    ssq_ref[...] = jnp.sum(x * x, axis=-1, keepdims=True)


def _prep(x, g):
    m, d = x.shape
    tm = _row_tile(m, ROW_TILE_NORM)
    (hg, ssq), _ = _call(
        _prep_kernel, grid=(m // tm,),
        in_specs=[pl.BlockSpec((tm, d), lambda i: (i, 0)), pl.BlockSpec((1, d), lambda i: (0, 0))],
        out_specs=[pl.BlockSpec((tm, d), lambda i: (i, 0)), pl.BlockSpec((tm, 1), lambda i: (i, 0))],
        out_shape=[jax.ShapeDtypeStruct((m, d), BF16), jax.ShapeDtypeStruct((m, 1), F32)],
        scratch=(), vmem=6 * tm * d * 4 + 4 * MIB, name="norm_inputs", args=(x, g.reshape(1, d)))
    return hg, ssq


def _rmsnorm_kernel(x_ref, g_ref, o_ref):
    x = x_ref[...]
    y = x * lax.rsqrt(jnp.mean(x * x, axis=-1, keepdims=True) + EPS)
    o_ref[...] = (y * g_ref[...]).astype(o_ref.dtype)


def _rmsnorm(x, g):
    m, d = x.shape
    tm = _row_tile(m, ROW_TILE_NORM)
    (y,), _ = _call(
        _rmsnorm_kernel, grid=(m // tm,),
        in_specs=[pl.BlockSpec((tm, d), lambda i: (i, 0)), pl.BlockSpec((1, d), lambda i: (0, 0))],
        out_specs=[pl.BlockSpec((tm, d), lambda i: (i, 0))],
        out_shape=[jax.ShapeDtypeStruct((m, d), F32)],
        scratch=(), vmem=7 * tm * d * 4 + 4 * MIB, name="rmsnorm", args=(x, g.reshape(1, d)))
    return y


def _retention_kernel(logg_ref, hg_ref, ssq_ref, w_ref, cos_ref, sin_ref, s0_ref, *rest,
                      tm, hd, chunk, emit_state):
    if emit_state:
        a_ref, sout_ref, state_ref, q_s, k_s, v_s, g_s = rest
    else:
        a_ref, state_ref, q_s, k_s, v_s, g_s = rest
    rb = pl.program_id(1)
    h = pl.program_id(2)
    half = hd // 2

    @pl.when(rb == 0)
    def _():
        state_ref[h] = s0_ref[...]

    lg = logg_ref[h]
    ri = lax.broadcasted_iota(jnp.int32, (chunk, chunk), 0)
    ci = lax.broadcasted_iota(jnp.int32, (chunk, chunk), 1)
    rel = (ri - ci).astype(F32)
    inner_decay = jnp.where(rel >= 0, jnp.exp(lg * jnp.maximum(rel, 0.0)), 0.0)
    jj = lax.broadcasted_iota(jnp.int32, (chunk, 1), 0).astype(F32)
    q_decay = jnp.exp(lg * (jj + 1.0))
    k_decay = jnp.exp(lg * (chunk - 1.0 - jj))
    chunk_decay = jnp.exp(jnp.full((1, hd), lg * chunk, F32))

    def project(rows):
        proj = _dot(hg_ref[rows, :], w_ref[...]) * _inv_rms(ssq_ref[rows, :], hg_ref.shape[1])
        cos = cos_ref[rows, :]
        sin = sin_ref[rows, :]

        def rot(t):
            t1, t2 = t[:, :half], t[:, half:]
            return jnp.concatenate([t1 * cos - t2 * sin, t1 * sin + t2 * cos], axis=-1)

        q_s[rows, :] = rot(proj[:, 0 * hd:1 * hd]).astype(BF16)
        k_s[rows, :] = rot(proj[:, 1 * hd:2 * hd]) * (hd ** -0.5)
        v_s[rows, :] = proj[:, 2 * hd:3 * hd].astype(BF16)
        g = proj[:, 3 * hd:4 * hd]
        g_s[rows, :] = g * jax.nn.sigmoid(g)

    def recur(c):
        rows = pl.ds(c * chunk, chunk)
        qc = q_s[rows, :]
        kc = k_s[rows, :]
        vc = v_s[rows, :]
        st = state_ref[h]
        scores = lax.dot_general(qc, kc.astype(BF16), (((1,), (1,)), ((), ())),
                                 preferred_element_type=F32) * inner_decay
        inner = _dot(scores.astype(BF16), vc)
        cross = _dot(qc, st.astype(BF16)) * q_decay
        state_ref[h] = st * chunk_decay + lax.dot_general(
            (kc * k_decay).astype(BF16), vc, (((0,), (0,)), ((), ())),
            preferred_element_type=F32)
        o = inner + cross
        o = o * lax.rsqrt(jnp.mean(o * o, axis=-1, keepdims=True) + EPS)
        a_ref[rows, :] = (o * g_s[rows, :]).astype(a_ref.dtype)

    gm = tm // ROW_GROUPS_RET if tm % (ROW_GROUPS_RET * chunk) == 0 else tm
    for r0 in range(0, tm, gm):
        project(pl.ds(r0, gm))
        for c in range(r0 // chunk, (r0 + gm) // chunk):
            recur(c)

    if emit_state:
        sout_ref[...] = state_ref[h]


def _retention(hg, ssq, w_ret, cos, sin, state0, log_gamma, *, batch, emit_state, casts=()):
    m, d = hg.shape
    lb = m // batch
    heads = RET_HEADS
    hd = w_ret.shape[1] // (4 * heads)
    tm = _row_tile(lb, ROW_TILE_RET)
    nrb = lb // tm
    chunk = RET_CHUNK if tm % RET_CHUNK == 0 else CHUNK
    out_shape = [jax.ShapeDtypeStruct((m, heads * hd), BF16)]
    out_specs = [pl.BlockSpec((tm, hd), lambda b, r, h: (b * nrb + r, h))]
    if emit_state:
        out_shape.append(jax.ShapeDtypeStruct((heads, hd, hd), F32))
        out_specs.append(pl.BlockSpec((None, hd, hd), lambda b, r, h: (h, 0, 0)))
    vmem = (2 * tm * d * 2 + 2 * d * 4 * hd * 2 + 3 * tm * 4 * hd * 4 // ROW_GROUPS_RET
            + heads * hd * hd * 4 + 6 * MIB)
    return _call(
        functools.partial(_retention_kernel, tm=tm, hd=hd, chunk=chunk, emit_state=emit_state),
        grid=(batch, nrb, heads),
        in_specs=[
            pl.BlockSpec(memory_space=pltpu.SMEM),
            pl.BlockSpec((tm, d), lambda b, r, h: (b * nrb + r, 0)),
            pl.BlockSpec((tm, 1), lambda b, r, h: (b * nrb + r, 0)),
            pl.BlockSpec((d, 4 * hd), lambda b, r, h: (0, h)),
            pl.BlockSpec((tm, hd // 2), lambda b, r, h: (r, 0)),
            pl.BlockSpec((tm, hd // 2), lambda b, r, h: (r, 0)),
            pl.BlockSpec((None, hd, hd), lambda b, r, h: (h, 0, 0)),
        ],
        out_specs=out_specs, out_shape=out_shape,
        scratch=[
            pltpu.VMEM((heads, hd, hd), F32),
            pltpu.VMEM((tm, hd), BF16),
            pltpu.VMEM((tm, hd), F32),
            pltpu.VMEM((tm, hd), BF16),
            pltpu.VMEM((tm, hd), F32),
        ],
        vmem=vmem, name="retention",
        args=(log_gamma, hg, ssq, w_ret, cos, sin, state0), casts=casts)


def _conv_kernel(hg_ref, ssq_ref, w_ref, cw_ref, halo_ref, *rest, tm, tn, nrb, emit_tail):
    if emit_tail:
        b1_ref, tail_ref, carry_ref = rest
    else:
        b1_ref, carry_ref = rest
    i = pl.program_id(0)
    j = pl.program_id(1)

    @pl.when(i % nrb == 0)
    def _():
        carry_ref[j] = halo_ref[...]

    proj = _dot(hg_ref[...], w_ref[...]) * _inv_rms(ssq_ref[...], hg_ref.shape[1])
    zc = proj[:, :tn] * proj[:, 2 * tn:]
    prev = carry_ref[j]
    p1 = prev[SUBLANES - 1:SUBLANES, :]
    p2 = prev[SUBLANES - 2:SUBLANES - 1, :]
    row = lax.broadcasted_iota(jnp.int32, zc.shape, 0)
    z1 = jnp.where(row == 0, p1, pltpu.roll(zc, 1, 0))
    z2 = jnp.where(row == 0, p2, jnp.where(row == 1, p1, pltpu.roll(zc, 2, 0)))
    cw = cw_ref[...]
    conv = cw[0:1, :] * z2 + cw[1:2, :] * z1 + cw[2:3, :] * zc
    b1_ref[...] = (proj[:, tn:2 * tn] * conv).astype(b1_ref.dtype)
    tail = zc[tm - SUBLANES:, :]
    carry_ref[j] = tail
    if emit_tail:
        tail_ref[...] = tail


def _conv_mixer(hg, ssq, w_cbz, conv_w, halo, *, batch, emit_tail, casts=()):
    m, d = hg.shape
    lb = m // batch
    dc = conv_w.shape[1]
    tn = _conv_tile(dc)
    tm = _row_tile(lb, ROW_TILE_CONV)
    nrb = lb // tm
    ni, nj = m // tm, dc // tn
    out_shape = [jax.ShapeDtypeStruct((m, dc), BF16)]
    out_specs = [pl.BlockSpec((tm, tn), lambda i, j: (i, j))]
    if emit_tail:
        out_shape.append(jax.ShapeDtypeStruct((ni * SUBLANES, dc), F32))
        out_specs.append(pl.BlockSpec((SUBLANES, tn), lambda i, j: (i, j)))
    vmem = 2 * tm * d * 2 + 2 * d * 3 * tn * 2 + 3 * tm * 3 * tn * 4 + 4 * MIB
    return _call(
        functools.partial(_conv_kernel, tm=tm, tn=tn, nrb=nrb, emit_tail=emit_tail),
        grid=(ni, nj),
        in_specs=[
            pl.BlockSpec((tm, d), lambda i, j: (i, 0)),
            pl.BlockSpec((tm, 1), lambda i, j: (i, 0)),
            pl.BlockSpec((d, 3 * tn), lambda i, j: (0, j)),
            pl.BlockSpec((CONV_WIDTH, tn), lambda i, j: (0, j)),
            pl.BlockSpec((SUBLANES, tn), lambda i, j: (0, j)),
        ],
        out_specs=out_specs, out_shape=out_shape,
        scratch=[pltpu.VMEM((nj, SUBLANES, tn), F32)],
        vmem=vmem, name="conv_mixer", args=(hg, ssq, w_cbz, conv_w, halo), casts=casts)


def _merge_kernel(a_ref, b_ref, hg_ref, ssq_ref, wro_ref, wco_ref, wg_ref, o_ref, *, tn):
    ya = _dot(a_ref[...], wro_ref[...])
    yb = _dot(b_ref[...], wco_ref[...])
    gates = _dot(hg_ref[...], wg_ref[...]) * _inv_rms(ssq_ref[...], hg_ref.shape[1])
    o_ref[...] = (jax.nn.sigmoid(gates[:, :tn]) * ya
                  + jax.nn.sigmoid(gates[:, tn:]) * yb).astype(o_ref.dtype)


def _merge(a1, b1, hg, ssq, w_ro, w_co, w_gate, *, casts=()):
    m, d = hg.shape
    dr = a1.shape[1]
    dc = b1.shape[1]
    tn = _merge_tile(d)
    tm = _row_tile(m, ROW_TILE_MERGE)
    vmem = (2 * tm * (dr + dc + d) * 2 + 2 * (dr + dc + 2 * d) * tn * 2
            + 8 * tm * tn * 4 + 2 * MIB)
    (merged,), cast_out = _call(
        functools.partial(_merge_kernel, tn=tn),
        grid=(m // tm, d // tn),
        in_specs=[
            pl.BlockSpec((tm, dr), lambda i, j: (i, 0)),
            pl.BlockSpec((tm, dc), lambda i, j: (i, 0)),
            pl.BlockSpec((tm, d), lambda i, j: (i, 0)),
            pl.BlockSpec((tm, 1), lambda i, j: (i, 0)),
            pl.BlockSpec((dr, tn), lambda i, j: (0, j)),
            pl.BlockSpec((dc, tn), lambda i, j: (0, j)),
            pl.BlockSpec((d, 2 * tn), lambda i, j: (0, j)),
        ],
        out_specs=[pl.BlockSpec((tm, tn), lambda i, j: (i, j))],
        out_shape=[jax.ShapeDtypeStruct((m, d), BF16)],
        scratch=(), vmem=vmem, name="merge",
        args=(a1, b1, hg, ssq, w_ro, w_co, w_gate), casts=casts)
    return merged, cast_out


def _residual_kernel(a_ref, w_ref, h_ref, *rest, nt, emit_scaled):
    if emit_scaled:
        g_ref, o_ref, hg_ref, ssq_ref, acc_ref = rest
    else:
        (o_ref,) = rest
    j = pl.program_id(1)
    if emit_scaled:
        @pl.when(j == 0)
        def _():
            acc_ref[...] = jnp.zeros_like(acc_ref)

    hn = h_ref[...] + _dot(a_ref[...], w_ref[...])
    o_ref[...] = hn
    if emit_scaled:
        hg_ref[...] = (hn * g_ref[...]).astype(hg_ref.dtype)
        sq = hn * hn
        part = sq[:, :LANES]
        for c in range(1, sq.shape[1] // LANES):
            part = part + sq[:, c * LANES:(c + 1) * LANES]
        acc_ref[...] += part

        @pl.when(j == nt - 1)
        def _():
            ssq_ref[...] = jnp.sum(acc_ref[...], axis=-1, keepdims=True)


def _residual(a, w_tiles, h, g_next, *, k_splits, casts=()):
    m, kdim = a.shape
    nt, _, tn = w_tiles.shape
    d = nt * tn
    kb = kdim // k_splits
    tm = _row_tile(m, ROW_TILE_RES)
    casts = casts or ((),) * k_splits
    hg = ssq = None
    cast_out = []
    for s in range(k_splits):
        emit_scaled = g_next is not None and s == k_splits - 1
        in_specs = [
            pl.BlockSpec((tm, kb), lambda i, j, s=s: (i, s)),
            pl.BlockSpec((None, kb, tn), lambda i, j, s=s: (j, s, 0)),
            pl.BlockSpec((tm, tn), lambda i, j: (i, j)),
        ]
        out_specs = [pl.BlockSpec((tm, tn), lambda i, j: (i, j))]
        out_shape = [jax.ShapeDtypeStruct((m, d), F32)]
        args = [a, w_tiles, h]
        scratch = []
        vmem = 2 * tm * kb * 2 + 2 * kb * tn * 2 + 5 * tm * tn * 4 + 4 * MIB
        if emit_scaled:
            in_specs.append(pl.BlockSpec((1, tn), lambda i, j: (0, j)))
            args.append(g_next.reshape(1, d))
            out_specs += [pl.BlockSpec((tm, tn), lambda i, j: (i, j)),
                          pl.BlockSpec((tm, 1), lambda i, j: (i, 0))]
            out_shape += [jax.ShapeDtypeStruct((m, d), BF16), jax.ShapeDtypeStruct((m, 1), F32)]
            scratch.append(pltpu.VMEM((tm, LANES), F32))
            vmem += 2 * tm * tn * 2 + 4 * tm * LANES * 4
        outs, cw = _call(
            functools.partial(_residual_kernel, nt=nt, emit_scaled=emit_scaled),
            grid=(m // tm, nt), in_specs=in_specs, out_specs=out_specs, out_shape=out_shape,
            scratch=scratch, vmem=vmem, name="residual", args=args, casts=casts[s])
        cast_out += cw
        h = outs[0]
        if emit_scaled:
            hg, ssq = outs[1], outs[2]
    return (h, hg, ssq), cast_out


def _up_kernel(hg_ref, ssq_ref, w_ref, o_ref):
    t = jnp.maximum(_dot(hg_ref[...], w_ref[...]) * _inv_rms(ssq_ref[...], hg_ref.shape[1]), 0.0)
    o_ref[...] = (t * t).astype(o_ref.dtype)


def _up(hg, ssq, w_up, *, casts=()):
    m, d = hg.shape
    f = w_up.shape[1]
    tm = _row_tile(m, ROW_TILE_UP)
    tn = min(f, COL_TILE_UP)
    vmem = 2 * tm * d * 2 + 2 * d * tn * 2 + 2 * tm * tn * 2 + 3 * tm * tn * 4 + 2 * MIB
    (hid,), cast_out = _call(
        _up_kernel, grid=(m // tm, f // tn),
        in_specs=[pl.BlockSpec((tm, d), lambda i, j: (i, 0)),
                  pl.BlockSpec((tm, 1), lambda i, j: (i, 0)),
                  pl.BlockSpec((d, tn), lambda i, j: (0, j))],
        out_specs=[pl.BlockSpec((tm, tn), lambda i, j: (i, j))],
        out_shape=[jax.ShapeDtypeStruct((m, f), BF16)],
        scratch=(), vmem=vmem, name="mlp_up", args=(hg, ssq, w_up), casts=casts)
    return hid, cast_out


def _rope_tables(pos, hd):
    inv_freq = ROPE_BASE ** (-jnp.arange(0, hd, 2, dtype=F32) / hd)
    ang = pos[:, None] * inv_freq[None, :]
    return jnp.cos(ang), jnp.sin(ang)


def kernel(x, meta_tokens, norm_mix_g, w_in, conv_w, w_ret_out, w_conv_out, w_out,
           norm_mlp_g, w_up, w_down, final_norm_g):
    batch, seq, d = x.shape
    depth = w_in.shape[0]
    n_meta = meta_tokens.shape[0]
    dr = w_ret_out.shape[1]
    dc = w_conv_out.shape[1]
    f = w_up.shape[2]
    hd = dr // RET_HEADS
    assert seq % CHUNK == 0 and n_meta <= CHUNK
    meta_pad = CHUNK - n_meta

    def cast(layer, name):
        spec = {
            "w_ret": (w_in, 0, 4, dr, hd, False),
            "w_cbz": (w_in, 4 * dr, 3, dc, _conv_tile(dc), False),
            "w_gate": (w_in, 4 * dr + 3 * dc, 2, d, _merge_tile(d), False),
            "w_ro": (w_ret_out, 0, 1, d, _merge_tile(d), False),
            "w_co": (w_conv_out, 0, 1, d, _merge_tile(d), False),
            "w_o": (w_out, 0, 1, d, min(d, COL_TILE_RES_O), True),
            "w_up": (w_up, 0, 1, f, min(f, COL_TILE_UP), False),
            "w_down": (w_down, 0, 1, d, min(d, COL_TILE_RES_DOWN), True),
        }[name]
        return _Cast(spec[0], layer, *spec[1:])

    lw = [dict() for _ in range(depth)]
    lw[0]["w_ret"] = _cast_now(cast(0, "w_ret"))

    def riding(targets):
        targets = [(layer, name) for layer, name in targets if layer < depth]
        return targets, tuple(cast(layer, name) for layer, name in targets)

    def keep(targets, cast_out):
        for (layer, name), w in zip(targets, cast_out):
            lw[layer][name] = w

    log_gamma = jnp.asarray(
        np.log1p(-np.exp2(-5.0 - np.arange(RET_HEADS, dtype=np.float64))), F32)
    cos_x, sin_x = _rope_tables(jnp.arange(seq, dtype=F32) + n_meta, hd)
    cos_m, sin_m = _rope_tables(jnp.arange(CHUNK, dtype=F32) - meta_pad, hd)

    hm = jnp.concatenate([jnp.zeros((meta_pad, d), F32), meta_tokens.astype(F32)], axis=0)
    meta = dict(h=hm, batch=1, cos=cos_m, sin=sin_m, state=jnp.zeros((RET_HEADS, hd, hd), F32),
                halo=jnp.zeros((SUBLANES, dc), F32), is_meta=True)
    real = dict(h=x.reshape(batch * seq, d), batch=batch, cos=cos_x, sin=sin_x, is_meta=False)
    for st in (meta, real):
        st["hg"], st["ssq"] = _prep(st["h"], norm_mix_g[0])

    def stage(streams, fn, *target_groups):
        groups = [riding(t) for t in target_groups]
        for st in streams:
            casts = [() if st["is_meta"] else c for _, c in groups]
            updates, out = fn(st, *casts)
            st.update(updates)
            if not st["is_meta"]:
                keep([t for targets, _ in groups for t in targets], out)

    for i in range(depth):
        w = lw[i]
        nxt = i + 1
        streams = (meta, real)

        def ret(st, casts):
            outs, out = _retention(st["hg"], st["ssq"], w["w_ret"], st["cos"], st["sin"],
                                   st["state"], log_gamma, batch=st["batch"],
                                   emit_state=st["is_meta"], casts=casts)
            if st["is_meta"]:
                real["state"] = outs[1]
            return dict(a1=outs[0]), out

        def conv(st, casts):
            outs, out = _conv_mixer(st["hg"], st["ssq"], w["w_cbz"], conv_w[i], st["halo"],
                                    batch=st["batch"], emit_tail=st["is_meta"], casts=casts)
            if st["is_meta"]:
                real["halo"] = outs[1]
            return dict(b1=outs[0]), out

        def merge(st, casts):
            merged, out = _merge(st["a1"], st["b1"], st["hg"], st["ssq"], w["w_ro"], w["w_co"],
                                 w["w_gate"], casts=casts)
            return dict(merged=merged), out

        def project_out(st, casts):
            (h, hg, ssq), out = _residual(st["merged"], w["w_o"], st["h"], norm_mlp_g[i],
                                          k_splits=1, casts=(casts,))
            return dict(h=h, hg=hg, ssq=ssq), out

        def up(st, casts):
            hid, out = _up(st["hg"], st["ssq"], w["w_up"], casts=casts)
            return dict(hid=hid), out

        def down(st, *casts):
            g_next = norm_mix_g[nxt] if nxt < depth else None
            (h, hg, ssq), out = _residual(st["hid"], w["w_down"], st["h"], g_next,
                                          k_splits=K_SPLITS_DOWN, casts=casts)
            return dict(h=h, hg=hg, ssq=ssq), out

        def own(*names):
            return [(i, n) for n in names]

        stage(streams, ret, own("w_cbz", "w_up"))
        stage(streams, conv, own("w_gate", "w_ro", "w_co", "w_down"))
        if nxt == depth:
            streams = (real,)
        stage(streams, merge, own("w_o"))
        stage(streams, project_out, [])
        stage(streams, up, [(nxt, "w_ret")])
        stage(streams, down, *[[] for _ in range(K_SPLITS_DOWN)])
    return _rmsnorm(real["h"], final_norm_g).reshape(batch, seq, d)
```

```python
import functools
from typing import NamedTuple

import numpy as np
import jax
import jax.numpy as jnp
from jax import lax
from jax.experimental import pallas as pl
from jax.experimental.pallas import tpu as pltpu

RET_HEADS = 8
CONV_WIDTH = 3
CHUNK = 128
ROPE_BASE = 10000.0
EPS = 1e-6

MIB = 1024 * 1024
V7X_VMEM_BYTES = 64 * MIB
VMEM_CAP_BYTES = V7X_VMEM_BYTES - 4 * MIB
SUBLANES = 8
LANES = 128

F32 = jnp.float32
BF16 = jnp.bfloat16

ROW_TILE_NORM = 512
ROW_TILE_RET = 1024
ROW_GROUPS_RET = 2
RET_CHUNK = 256
ROW_TILE_CONV = 1024
COL_TILE_CONV = 256
ROW_TILE_MERGE = 512
COL_TILE_MERGE = 512
ROW_TILE_RES = 1024
COL_TILE_RES_O = 512
COL_TILE_RES_DOWN = 256
K_SPLITS_DOWN = 2
ROW_TILE_UP = 1024
COL_TILE_UP = 1024
CAST_BLOCK_ELEMS = 1024 * 1024
MIN_CAST_BLOCK_ELEMS = 64 * 1024


def _dot(a, b):
    return jnp.dot(a, b, preferred_element_type=F32)


def _row_tile(m, want):
    tm = min(m, want)
    assert m % tm == 0, (m, tm)
    return tm


def _conv_tile(dc):
    return min(dc, COL_TILE_CONV)


def _merge_tile(d):
    return min(d, COL_TILE_MERGE)


def _inv_rms(ssq, d):
    return lax.rsqrt(ssq / d + EPS)


class _Cast(NamedTuple):
    w: jax.Array
    layer: int
    col0: int
    sections: int
    width: int
    tile: int
    tile_major: bool
    block_elems: int = 0


def _cast_geometry(c):
    kdim = c.w.shape[1]
    tiles_per_section = c.width // c.tile
    n_tiles = c.sections * tiles_per_section
    tk = min(kdim, c.block_elems // c.tile)
    return kdim, tiles_per_section, n_tiles, tk, kdim // tk


def _fit_blocks(c, n_steps):
    c = c._replace(block_elems=MIN_CAST_BLOCK_ELEMS)
    while _cast_geometry(c)[2] * _cast_geometry(c)[4] > n_steps:
        assert _cast_geometry(c)[3] < c.w.shape[1], (c.block_elems, n_steps)
        c = c._replace(block_elems=2 * c.block_elems)
    return c


def _cast_specs(c, step_of):
    kdim, tiles_per_section, n_tiles, tk, kblocks = _cast_geometry(c)
    n_blocks = n_tiles * kblocks
    base = c.col0 // c.tile

    def tile_and_kblock(*idx):
        t = jnp.minimum(step_of(*idx), n_blocks - 1)
        return t // kblocks, t % kblocks

    def src(*idx):
        p, kk = tile_and_kblock(*idx)
        return c.layer, kk, base + (p % c.sections) * tiles_per_section + p // c.sections

    if c.tile_major:
        shape = (n_tiles, kdim, c.tile)
        block = (None, tk, c.tile)

        def dst(*idx):
            p, kk = tile_and_kblock(*idx)
            return p, kk, 0
    else:
        shape = (kdim, n_tiles * c.tile)
        block = (tk, c.tile)

        def dst(*idx):
            p, kk = tile_and_kblock(*idx)
            return kk, p

    return (pl.BlockSpec((None, tk, c.tile), src), pl.BlockSpec(block, dst),
            jax.ShapeDtypeStruct(shape, BF16), n_blocks)


def _call(body, *, grid, in_specs, out_specs, out_shape, scratch, vmem, name, args, casts=()):
    n_in, n_out, n_cast = len(in_specs), len(out_specs), len(casts)

    def step_of(*idx):
        t = idx[0]
        for size, i in zip(grid[1:], idx[1:]):
            t = t * size + i
        return t

    n_steps = int(np.prod(grid))
    casts = [c if c.block_elems else _fit_blocks(c, n_steps) for c in casts]
    specs = [_cast_specs(c, step_of) for c in casts]
    for c, s in zip(casts, specs):
        assert s[3] <= n_steps, (name, s[3], n_steps)
        vmem += 2 * c.block_elems * (4 + 2)

    def kernel(*refs):
        ins = refs[:n_in]
        cast_in = refs[n_in:n_in + n_cast]
        outs = refs[n_in + n_cast:n_in + n_cast + n_out]
        cast_out = refs[n_in + n_cast + n_out:n_in + 2 * n_cast + n_out]
        for x_ref, o_ref in zip(cast_in, cast_out):
            o_ref[...] = x_ref[...].astype(o_ref.dtype)
        body(*ins, *outs, *refs[n_in + 2 * n_cast + n_out:])

    assert vmem <= VMEM_CAP_BYTES, (name, vmem)
    res = pl.pallas_call(
        kernel,
        grid=grid,
        in_specs=[*in_specs, *[s[0] for s in specs]],
        out_specs=[*out_specs, *[s[1] for s in specs]],
        out_shape=[*out_shape, *[s[2] for s in specs]],
        scratch_shapes=list(scratch),
        compiler_params=pltpu.CompilerParams(
            dimension_semantics=("arbitrary",) * len(grid), vmem_limit_bytes=int(vmem)),
        name=name,
    )(*args, *[c.w for c in casts])
    return list(res[:n_out]), list(res[n_out:])


def _cast_now(c):
    c = c._replace(block_elems=CAST_BLOCK_ELEMS)
    n_tiles, kblocks = _cast_geometry(c)[2], _cast_geometry(c)[4]
    _, (w,) = _call(lambda: None, grid=(n_tiles * kblocks,), in_specs=[], out_specs=[],
                    out_shape=[], scratch=(), vmem=4 * MIB, name="cast_weight", args=(),
                    casts=(c,))
    return w


def _prep_kernel(x_ref, g_ref, hg_ref, ssq_ref):
    x = x_ref[...]
    hg_ref[...] = (x * g_ref[...]).astype(hg_ref.dtype)
    ssq_ref[...] = jnp.sum(x * x, axis=-1, keepdims=True)


def _prep(x, g):
    m, d = x.shape
    tm = _row_tile(m, ROW_TILE_NORM)
    (hg, ssq), _ = _call(
        _prep_kernel, grid=(m // tm,),
        in_specs=[pl.BlockSpec((tm, d), lambda i: (i, 0)), pl.BlockSpec((1, d), lambda i: (0, 0))],
        out_specs=[pl.BlockSpec((tm, d), lambda i: (i, 0)), pl.BlockSpec((tm, 1), lambda i: (i, 0))],
        out_shape=[jax.ShapeDtypeStruct((m, d), BF16), jax.ShapeDtypeStruct((m, 1), F32)],
        scratch=(), vmem=6 * tm * d * 4 + 4 * MIB, name="norm_inputs", args=(x, g.reshape(1, d)))
    return hg, ssq


def _rmsnorm_kernel(x_ref, g_ref, o_ref):
    x = x_ref[...]
    y = x * lax.rsqrt(jnp.mean(x * x, axis=-1, keepdims=True) + EPS)
    o_ref[...] = (y * g_ref[...]).astype(o_ref.dtype)


def _rmsnorm(x, g):
    m, d = x.shape
    tm = _row_tile(m, ROW_TILE_NORM)
    (y,), _ = _call(
        _rmsnorm_kernel, grid=(m // tm,),
        in_specs=[pl.BlockSpec((tm, d), lambda i: (i, 0)), pl.BlockSpec((1, d), lambda i: (0, 0))],
        out_specs=[pl.BlockSpec((tm, d), lambda i: (i, 0))],
        out_shape=[jax.ShapeDtypeStruct((m, d), F32)],
        scratch=(), vmem=7 * tm * d * 4 + 4 * MIB, name="rmsnorm", args=(x, g.reshape(1, d)))
    return y


def _retention_kernel(logg_ref, hg_ref, ssq_ref, w_ref, cos_ref, sin_ref, s0_ref, *rest,
                      tm, hd, chunk, emit_state):
    if emit_state:
        a_ref, sout_ref, state_ref, q_s, k_s, v_s, g_s = rest
    else:
        a_ref, state_ref, q_s, k_s, v_s, g_s = rest
    rb = pl.program_id(1)
    h = pl.program_id(2)
    half = hd // 2

    @pl.when(rb == 0)
    def _():
        state_ref[h] = s0_ref[...]

    lg = logg_ref[h]
    ri = lax.broadcasted_iota(jnp.int32, (chunk, chunk), 0)
    ci = lax.broadcasted_iota(jnp.int32, (chunk, chunk), 1)
    rel = (ri - ci).astype(F32)
    inner_decay = jnp.where(rel >= 0, jnp.exp(lg * jnp.maximum(rel, 0.0)), 0.0)
    jj = lax.broadcasted_iota(jnp.int32, (chunk, 1), 0).astype(F32)
    q_decay = jnp.exp(lg * (jj + 1.0))
    k_decay = jnp.exp(lg * (chunk - 1.0 - jj))
    chunk_decay = jnp.exp(jnp.full((1, hd), lg * chunk, F32))

    def project(rows):
        proj = _dot(hg_ref[rows, :], w_ref[...]) * _inv_rms(ssq_ref[rows, :], hg_ref.shape[1])
        cos = cos_ref[rows, :]
        sin = sin_ref[rows, :]

        def rot(t):
            t1, t2 = t[:, :half], t[:, half:]
            return jnp.concatenate([t1 * cos - t2 * sin, t1 * sin + t2 * cos], axis=-1)

        q_s[rows, :] = rot(proj[:, 0 * hd:1 * hd]).astype(BF16)
        k_s[rows, :] = rot(proj[:, 1 * hd:2 * hd]) * (hd ** -0.5)
        v_s[rows, :] = proj[:, 2 * hd:3 * hd].astype(BF16)
        g = proj[:, 3 * hd:4 * hd]
        g_s[rows, :] = g * jax.nn.sigmoid(g)

    def recur(c):
        rows = pl.ds(c * chunk, chunk)
        qc = q_s[rows, :]
        kc = k_s[rows, :]
        vc = v_s[rows, :]
        st = state_ref[h]
        scores = lax.dot_general(qc, kc.astype(BF16), (((1,), (1,)), ((), ())),
                                 preferred_element_type=F32) * inner_decay
        inner = _dot(scores.astype(BF16), vc)
        cross = _dot(qc, st.astype(BF16)) * q_decay
        state_ref[h] = st * chunk_decay + lax.dot_general(
            (kc * k_decay).astype(BF16), vc, (((0,), (0,)), ((), ())),
            preferred_element_type=F32)
        o = inner + cross
        o = o * lax.rsqrt(jnp.mean(o * o, axis=-1, keepdims=True) + EPS)
        a_ref[rows, :] = (o * g_s[rows, :]).astype(a_ref.dtype)

    gm = tm // ROW_GROUPS_RET if tm % (ROW_GROUPS_RET * chunk) == 0 else tm
    for r0 in range(0, tm, gm):
        project(pl.ds(r0, gm))
        for c in range(r0 // chunk, (r0 + gm) // chunk):
            recur(c)

    if emit_state:
        sout_ref[...] = state_ref[h]


def _retention(hg, ssq, w_ret, cos, sin, state0, log_gamma, *, batch, emit_state, casts=()):
    m, d = hg.shape
    lb = m // batch
    heads = RET_HEADS
    hd = w_ret.shape[1] // (4 * heads)
    tm = _row_tile(lb, ROW_TILE_RET)
    nrb = lb // tm
    chunk = RET_CHUNK if tm % RET_CHUNK == 0 else CHUNK
    out_shape = [jax.ShapeDtypeStruct((m, heads * hd), BF16)]
    out_specs = [pl.BlockSpec((tm, hd), lambda b, r, h: (b * nrb + r, h))]
    if emit_state:
        out_shape.append(jax.ShapeDtypeStruct((heads, hd, hd), F32))
        out_specs.append(pl.BlockSpec((None, hd, hd), lambda b, r, h: (h, 0, 0)))
    vmem = (2 * tm * d * 2 + 2 * d * 4 * hd * 2 + 3 * tm * 4 * hd * 4 // ROW_GROUPS_RET
            + heads * hd * hd * 4 + 6 * MIB)
    return _call(
        functools.partial(_retention_kernel, tm=tm, hd=hd, chunk=chunk, emit_state=emit_state),
        grid=(batch, nrb, heads),
        in_specs=[
            pl.BlockSpec(memory_space=pltpu.SMEM),
            pl.BlockSpec((tm, d), lambda b, r, h: (b * nrb + r, 0)),
            pl.BlockSpec((tm, 1), lambda b, r, h: (b * nrb + r, 0)),
            pl.BlockSpec((d, 4 * hd), lambda b, r, h: (0, h)),
            pl.BlockSpec((tm, hd // 2), lambda b, r, h: (r, 0)),
            pl.BlockSpec((tm, hd // 2), lambda b, r, h: (r, 0)),
            pl.BlockSpec((None, hd, hd), lambda b, r, h: (h, 0, 0)),
        ],
        out_specs=out_specs, out_shape=out_shape,
        scratch=[
            pltpu.VMEM((heads, hd, hd), F32),
            pltpu.VMEM((tm, hd), BF16),
            pltpu.VMEM((tm, hd), F32),
            pltpu.VMEM((tm, hd), BF16),
            pltpu.VMEM((tm, hd), F32),
        ],
        vmem=vmem, name="retention",
        args=(log_gamma, hg, ssq, w_ret, cos, sin, state0), casts=casts)


def _conv_kernel(hg_ref, ssq_ref, w_ref, cw_ref, halo_ref, *rest, tm, tn, nrb, emit_tail):
    if emit_tail:
        b1_ref, tail_ref, carry_ref = rest
    else:
        b1_ref, carry_ref = rest
    i = pl.program_id(0)
    j = pl.program_id(1)

    @pl.when(i % nrb == 0)
    def _():
        carry_ref[j] = halo_ref[...]

    proj = _dot(hg_ref[...], w_ref[...]) * _inv_rms(ssq_ref[...], hg_ref.shape[1])
    zc = proj[:, :tn] * proj[:, 2 * tn:]
    prev = carry_ref[j]
    p1 = prev[SUBLANES - 1:SUBLANES, :]
    p2 = prev[SUBLANES - 2:SUBLANES - 1, :]
    row = lax.broadcasted_iota(jnp.int32, zc.shape, 0)
    z1 = jnp.where(row == 0, p1, pltpu.roll(zc, 1, 0))
    z2 = jnp.where(row == 0, p2, jnp.where(row == 1, p1, pltpu.roll(zc, 2, 0)))
    cw = cw_ref[...]
    conv = cw[0:1, :] * z2 + cw[1:2, :] * z1 + cw[2:3, :] * zc
    b1_ref[...] = (proj[:, tn:2 * tn] * conv).astype(b1_ref.dtype)
    tail = zc[tm - SUBLANES:, :]
    carry_ref[j] = tail
    if emit_tail:
        tail_ref[...] = tail


def _conv_mixer(hg, ssq, w_cbz, conv_w, halo, *, batch, emit_tail, casts=()):
    m, d = hg.shape
    lb = m // batch
    dc = conv_w.shape[1]
    tn = _conv_tile(dc)
    tm = _row_tile(lb, ROW_TILE_CONV)
    nrb = lb // tm
    ni, nj = m // tm, dc // tn
    out_shape = [jax.ShapeDtypeStruct((m, dc), BF16)]
    out_specs = [pl.BlockSpec((tm, tn), lambda i, j: (i, j))]
    if emit_tail:
        out_shape.append(jax.ShapeDtypeStruct((ni * SUBLANES, dc), F32))
        out_specs.append(pl.BlockSpec((SUBLANES, tn), lambda i, j: (i, j)))
    vmem = 2 * tm * d * 2 + 2 * d * 3 * tn * 2 + 3 * tm * 3 * tn * 4 + 4 * MIB
    return _call(
        functools.partial(_conv_kernel, tm=tm, tn=tn, nrb=nrb, emit_tail=emit_tail),
        grid=(ni, nj),
        in_specs=[
            pl.BlockSpec((tm, d), lambda i, j: (i, 0)),
            pl.BlockSpec((tm, 1), lambda i, j: (i, 0)),
            pl.BlockSpec((d, 3 * tn), lambda i, j: (0, j)),
            pl.BlockSpec((CONV_WIDTH, tn), lambda i, j: (0, j)),
            pl.BlockSpec((SUBLANES, tn), lambda i, j: (0, j)),
        ],
        out_specs=out_specs, out_shape=out_shape,
        scratch=[pltpu.VMEM((nj, SUBLANES, tn), F32)],
        vmem=vmem, name="conv_mixer", args=(hg, ssq, w_cbz, conv_w, halo), casts=casts)


def _merge_kernel(a_ref, b_ref, hg_ref, ssq_ref, wro_ref, wco_ref, wg_ref, o_ref, *, tn):
    ya = _dot(a_ref[...], wro_ref[...])
    yb = _dot(b_ref[...], wco_ref[...])
    gates = _dot(hg_ref[...], wg_ref[...]) * _inv_rms(ssq_ref[...], hg_ref.shape[1])
    o_ref[...] = (jax.nn.sigmoid(gates[:, :tn]) * ya
                  + jax.nn.sigmoid(gates[:, tn:]) * yb).astype(o_ref.dtype)


def _merge(a1, b1, hg, ssq, w_ro, w_co, w_gate, *, casts=()):
    m, d = hg.shape
    dr = a1.shape[1]
    dc = b1.shape[1]
    tn = _merge_tile(d)
    tm = _row_tile(m, ROW_TILE_MERGE)
    vmem = (2 * tm * (dr + dc + d) * 2 + 2 * (dr + dc + 2 * d) * tn * 2
            + 8 * tm * tn * 4 + 2 * MIB)
    (merged,), cast_out = _call(
        functools.partial(_merge_kernel, tn=tn),
        grid=(m // tm, d // tn),
        in_specs=[
            pl.BlockSpec((tm, dr), lambda i, j: (i, 0)),
            pl.BlockSpec((tm, dc), lambda i, j: (i, 0)),
            pl.BlockSpec((tm, d), lambda i, j: (i, 0)),
            pl.BlockSpec((tm, 1), lambda i, j: (i, 0)),
            pl.BlockSpec((dr, tn), lambda i, j: (0, j)),
            pl.BlockSpec((dc, tn), lambda i, j: (0, j)),
            pl.BlockSpec((d, 2 * tn), lambda i, j: (0, j)),
        ],
        out_specs=[pl.BlockSpec((tm, tn), lambda i, j: (i, j))],
        out_shape=[jax.ShapeDtypeStruct((m, d), BF16)],
        scratch=(), vmem=vmem, name="merge",
        args=(a1, b1, hg, ssq, w_ro, w_co, w_gate), casts=casts)
    return merged, cast_out


def _residual_kernel(a_ref, w_ref, h_ref, *rest, nt, emit_scaled):
    if emit_scaled:
        g_ref, o_ref, hg_ref, ssq_ref, acc_ref = rest
    else:
        (o_ref,) = rest
    j = pl.program_id(1)
    if emit_scaled:
        @pl.when(j == 0)
        def _():
            acc_ref[...] = jnp.zeros_like(acc_ref)

    hn = h_ref[...] + _dot(a_ref[...], w_ref[...])
    o_ref[...] = hn
    if emit_scaled:
        hg_ref[...] = (hn * g_ref[...]).astype(hg_ref.dtype)
        sq = hn * hn
        part = sq[:, :LANES]
        for c in range(1, sq.shape[1] // LANES):
            part = part + sq[:, c * LANES:(c + 1) * LANES]
        acc_ref[...] += part

        @pl.when(j == nt - 1)
        def _():
            ssq_ref[...] = jnp.sum(acc_ref[...], axis=-1, keepdims=True)


def _residual(a, w_tiles, h, g_next, *, k_splits, casts=()):
    m, kdim = a.shape
    nt, _, tn = w_tiles.shape
    d = nt * tn
    kb = kdim // k_splits
    tm = _row_tile(m, ROW_TILE_RES)
    casts = casts or ((),) * k_splits
    hg = ssq = None
    cast_out = []
    for s in range(k_splits):
        emit_scaled = g_next is not None and s == k_splits - 1
        in_specs = [
            pl.BlockSpec((tm, kb), lambda i, j, s=s: (i, s)),
            pl.BlockSpec((None, kb, tn), lambda i, j, s=s: (j, s, 0)),
            pl.BlockSpec((tm, tn), lambda i, j: (i, j)),
        ]
        out_specs = [pl.BlockSpec((tm, tn), lambda i, j: (i, j))]
        out_shape = [jax.ShapeDtypeStruct((m, d), F32)]
        args = [a, w_tiles, h]
        scratch = []
        vmem = 2 * tm * kb * 2 + 2 * kb * tn * 2 + 5 * tm * tn * 4 + 4 * MIB
        if emit_scaled:
            in_specs.append(pl.BlockSpec((1, tn), lambda i, j: (0, j)))
            args.append(g_next.reshape(1, d))
            out_specs += [pl.BlockSpec((tm, tn), lambda i, j: (i, j)),
                          pl.BlockSpec((tm, 1), lambda i, j: (i, 0))]
            out_shape += [jax.ShapeDtypeStruct((m, d), BF16), jax.ShapeDtypeStruct((m, 1), F32)]
            scratch.append(pltpu.VMEM((tm, LANES), F32))
            vmem += 2 * tm * tn * 2 + 4 * tm * LANES * 4
        outs, cw = _call(
            functools.partial(_residual_kernel, nt=nt, emit_scaled=emit_scaled),
            grid=(m // tm, nt), in_specs=in_specs, out_specs=out_specs, out_shape=out_shape,
            scratch=scratch, vmem=vmem, name="residual", args=args, casts=casts[s])
        cast_out += cw
        h = outs[0]
        if emit_scaled:
            hg, ssq = outs[1], outs[2]
    return (h, hg, ssq), cast_out


def _up_kernel(hg_ref, ssq_ref, w_ref, o_ref):
    t = jnp.maximum(_dot(hg_ref[...], w_ref[...]) * _inv_rms(ssq_ref[...], hg_ref.shape[1]), 0.0)
    o_ref[...] = (t * t).astype(o_ref.dtype)


def _up(hg, ssq, w_up, *, casts=()):
    m, d = hg.shape
    f = w_up.shape[1]
    tm = _row_tile(m, ROW_TILE_UP)
    tn = min(f, COL_TILE_UP)
    vmem = 2 * tm * d * 2 + 2 * d * tn * 2 + 2 * tm * tn * 2 + 3 * tm * tn * 4 + 2 * MIB
    (hid,), cast_out = _call(
        _up_kernel, grid=(m // tm, f // tn),
        in_specs=[pl.BlockSpec((tm, d), lambda i, j: (i, 0)),
                  pl.BlockSpec((tm, 1), lambda i, j: (i, 0)),
                  pl.BlockSpec((d, tn), lambda i, j: (0, j))],
        out_specs=[pl.BlockSpec((tm, tn), lambda i, j: (i, j))],
        out_shape=[jax.ShapeDtypeStruct((m, f), BF16)],
        scratch=(), vmem=vmem, name="mlp_up", args=(hg, ssq, w_up), casts=casts)
    return hid, cast_out


def _rope_tables(pos, hd):
    inv_freq = ROPE_BASE ** (-jnp.arange(0, hd, 2, dtype=F32) / hd)
    ang = pos[:, None] * inv_freq[None, :]
    return jnp.cos(ang), jnp.sin(ang)


def kernel(x, meta_tokens, norm_mix_g, w_in, conv_w, w_ret_out, w_conv_out, w_out,
           norm_mlp_g, w_up, w_down, final_norm_g):
    batch, seq, d = x.shape
    depth = w_in.shape[0]
    n_meta = meta_tokens.shape[0]
    dr = w_ret_out.shape[1]
    dc = w_conv_out.shape[1]
    f = w_up.shape[2]
    hd = dr // RET_HEADS
    assert seq % CHUNK == 0 and n_meta <= CHUNK
    meta_pad = CHUNK - n_meta

    def cast(layer, name):
        spec = {
            "w_ret": (w_in, 0, 4, dr, hd, False),
            "w_cbz": (w_in, 4 * dr, 3, dc, _conv_tile(dc), False),
            "w_gate": (w_in, 4 * dr + 3 * dc, 2, d, _merge_tile(d), False),
            "w_ro": (w_ret_out, 0, 1, d, _merge_tile(d), False),
            "w_co": (w_conv_out, 0, 1, d, _merge_tile(d), False),
            "w_o": (w_out, 0, 1, d, min(d, COL_TILE_RES_O), True),
            "w_up": (w_up, 0, 1, f, min(f, COL_TILE_UP), False),
            "w_down": (w_down, 0, 1, d, min(d, COL_TILE_RES_DOWN), True),
        }[name]
        return _Cast(spec[0], layer, *spec[1:])

    lw = [dict() for _ in range(depth)]
    lw[0]["w_ret"] = _cast_now(cast(0, "w_ret"))

    def riding(targets):
        targets = [(layer, name) for layer, name in targets if layer < depth]
        return targets, tuple(cast(layer, name) for layer, name in targets)

    def keep(targets, cast_out):
        for (layer, name), w in zip(targets, cast_out):
            lw[layer][name] = w

    log_gamma = jnp.asarray(
        np.log1p(-np.exp2(-5.0 - np.arange(RET_HEADS, dtype=np.float64))), F32)
    cos_x, sin_x = _rope_tables(jnp.arange(seq, dtype=F32) + n_meta, hd)
    cos_m, sin_m = _rope_tables(jnp.arange(CHUNK, dtype=F32) - meta_pad, hd)

    hm = jnp.concatenate([jnp.zeros((meta_pad, d), F32), meta_tokens.astype(F32)], axis=0)
    meta = dict(h=hm, batch=1, cos=cos_m, sin=sin_m, state=jnp.zeros((RET_HEADS, hd, hd), F32),
                halo=jnp.zeros((SUBLANES, dc), F32), is_meta=True)
    real = dict(h=x.reshape(batch * seq, d), batch=batch, cos=cos_x, sin=sin_x, is_meta=False)
    for st in (meta, real):
        st["hg"], st["ssq"] = _prep(st["h"], norm_mix_g[0])

    def stage(streams, fn, *target_groups):
        groups = [riding(t) for t in target_groups]
        for st in streams:
            casts = [() if st["is_meta"] else c for _, c in groups]
            updates, out = fn(st, *casts)
            st.update(updates)
            if not st["is_meta"]:
                keep([t for targets, _ in groups for t in targets], out)

    for i in range(depth):
        w = lw[i]
        nxt = i + 1
        streams = (meta, real)

        def ret(st, casts):
            outs, out = _retention(st["hg"], st["ssq"], w["w_ret"], st["cos"], st["sin"],
                                   st["state"], log_gamma, batch=st["batch"],
                                   emit_state=st["is_meta"], casts=casts)
            if st["is_meta"]:
                real["state"] = outs[1]
            return dict(a1=outs[0]), out

        def conv(st, casts):
            outs, out = _conv_mixer(st["hg"], st["ssq"], w["w_cbz"], conv_w[i], st["halo"],
                                    batch=st["batch"], emit_tail=st["is_meta"], casts=casts)
            if st["is_meta"]:
                real["halo"] = outs[1]
            return dict(b1=outs[0]), out

        def merge(st, casts):
            merged, out = _merge(st["a1"], st["b1"], st["hg"], st["ssq"], w["w_ro"], w["w_co"],
                                 w["w_gate"], casts=casts)
            return dict(merged=merged), out

        def project_out(st, casts):
            (h, hg, ssq), out = _residual(st["merged"], w["w_o"], st["h"], norm_mlp_g[i],
                                          k_splits=1, casts=(casts,))
            return dict(h=h, hg=hg, ssq=ssq), out

        def up(st, casts):
            hid, out = _up(st["hg"], st["ssq"], w["w_up"], casts=casts)
            return dict(hid=hid), out

        def down(st, *casts):
            g_next = norm_mix_g[nxt] if nxt < depth else None
            (h, hg, ssq), out = _residual(st["hid"], w["w_down"], st["h"], g_next,
                                          k_splits=K_SPLITS_DOWN, casts=casts)
            return dict(h=h, hg=hg, ssq=ssq), out

        def own(*names):
            return [(i, n) for n in names]

        stage(streams, ret, own("w_cbz", "w_up"))
        stage(streams, conv, own("w_gate", "w_ro", "w_co", "w_o", "w_down") + [(nxt, "w_ret")])
        if nxt == depth:
            streams = (real,)
        stage(streams, merge, [])
        stage(streams, project_out, [])
        stage(streams, up, [])
        stage(streams, down, *[[] for _ in range(K_SPLITS_DOWN)])
    return _rmsnorm(real["h"], final_norm_g).reshape(batch, seq, d)
```

```python
import functools
from typing import NamedTuple

import numpy as np
import jax
import jax.numpy as jnp
from jax import lax
from jax.experimental import pallas as pl
from jax.experimental.pallas import tpu as pltpu

RET_HEADS = 8
CONV_WIDTH = 3
CHUNK = 128
ROPE_BASE = 10000.0
EPS = 1e-6

MIB = 1024 * 1024
V7X_VMEM_BYTES = 64 * MIB
VMEM_CAP_BYTES = V7X_VMEM_BYTES - 4 * MIB
SUBLANES = 8
LANES = 128

F32 = jnp.float32
BF16 = jnp.bfloat16

ROW_TILE_NORM = 512
ROW_TILE_RET = 1024
ROW_GROUPS_RET = 2
RET_CHUNK = 256
ROW_TILE_CONV = 1024
COL_TILE_CONV = 256
ROW_TILE_MERGE = 512
COL_TILE_MERGE = 512
ROW_TILE_RES = 1024
COL_TILE_RES_O = 512
COL_TILE_RES_DOWN = 256
K_SPLITS_DOWN = 2
ROW_TILE_UP = 1024
COL_TILE_UP = 1024
MIN_CAST_BLOCK_ELEMS = 64 * 1024


def _dot(a, b):
    return jnp.dot(a, b, preferred_element_type=F32)


def _row_tile(m, want):
    tm = min(m, want)
    assert m % tm == 0, (m, tm)
    return tm


def _conv_tile(dc):
    return min(dc, COL_TILE_CONV)


def _merge_tile(d):
    return min(d, COL_TILE_MERGE)


def _inv_rms(ssq, d):
    return lax.rsqrt(ssq / d + EPS)


class _Cast(NamedTuple):
    w: jax.Array
    layer: int
    col0: int
    sections: int
    width: int
    tile: int
    tile_major: bool
    block_elems: int = 0


def _cast_geometry(c):
    kdim = c.w.shape[1]
    tiles_per_section = c.width // c.tile
    n_tiles = c.sections * tiles_per_section
    tk = min(kdim, c.block_elems // c.tile)
    return kdim, tiles_per_section, n_tiles, tk, kdim // tk


def _fit_blocks(c, n_steps):
    c = c._replace(block_elems=MIN_CAST_BLOCK_ELEMS)
    while _cast_geometry(c)[2] * _cast_geometry(c)[4] > n_steps:
        assert _cast_geometry(c)[3] < c.w.shape[1], (c.block_elems, n_steps)
        c = c._replace(block_elems=2 * c.block_elems)
    return c


def _cast_specs(c, step_of):
    kdim, tiles_per_section, n_tiles, tk, kblocks = _cast_geometry(c)
    n_blocks = n_tiles * kblocks
    base = c.col0 // c.tile

    def tile_and_kblock(*idx):
        t = jnp.minimum(step_of(*idx), n_blocks - 1)
        return t // kblocks, t % kblocks

    def src(*idx):
        p, kk = tile_and_kblock(*idx)
        return c.layer, kk, base + (p % c.sections) * tiles_per_section + p // c.sections

    if c.tile_major:
        shape = (n_tiles, kdim, c.tile)
        block = (None, tk, c.tile)

        def dst(*idx):
            p, kk = tile_and_kblock(*idx)
            return p, kk, 0
    else:
        shape = (kdim, n_tiles * c.tile)
        block = (tk, c.tile)

        def dst(*idx):
            p, kk = tile_and_kblock(*idx)
            return kk, p

    return (pl.BlockSpec((None, tk, c.tile), src), pl.BlockSpec(block, dst),
            jax.ShapeDtypeStruct(shape, BF16), n_blocks)


def _call(body, *, grid, in_specs, out_specs, out_shape, scratch, vmem, name, args, casts=()):
    n_in, n_out, n_cast = len(in_specs), len(out_specs), len(casts)

    def step_of(*idx):
        t = idx[0]
        for size, i in zip(grid[1:], idx[1:]):
            t = t * size + i
        return t

    n_steps = int(np.prod(grid))
    casts = [c if c.block_elems else _fit_blocks(c, n_steps) for c in casts]
    specs = [_cast_specs(c, step_of) for c in casts]
    for c, s in zip(casts, specs):
        assert s[3] <= n_steps, (name, s[3], n_steps)
        vmem += 2 * c.block_elems * (4 + 2)

    def kernel(*refs):
        ins = refs[:n_in]
        cast_in = refs[n_in:n_in + n_cast]
        outs = refs[n_in + n_cast:n_in + n_cast + n_out]
        cast_out = refs[n_in + n_cast + n_out:n_in + 2 * n_cast + n_out]
        for x_ref, o_ref in zip(cast_in, cast_out):
            o_ref[...] = x_ref[...].astype(o_ref.dtype)
        body(*ins, *outs, *refs[n_in + 2 * n_cast + n_out:])

    assert vmem <= VMEM_CAP_BYTES, (name, vmem)
    res = pl.pallas_call(
        kernel,
        grid=grid,
        in_specs=[*in_specs, *[s[0] for s in specs]],
        out_specs=[*out_specs, *[s[1] for s in specs]],
        out_shape=[*out_shape, *[s[2] for s in specs]],
        scratch_shapes=list(scratch),
        compiler_params=pltpu.CompilerParams(
            dimension_semantics=("arbitrary",) * len(grid), vmem_limit_bytes=int(vmem)),
        name=name,
    )(*args, *[c.w for c in casts])
    return list(res[:n_out]), list(res[n_out:])


def _prep_kernel(x_ref, g_ref, hg_ref, ssq_ref):
    x = x_ref[...]
    hg_ref[...] = (x * g_ref[...]).astype(hg_ref.dtype)
    ssq_ref[...] = jnp.sum(x * x, axis=-1, keepdims=True)


def _prep(x, g, *, casts=()):
    m, d = x.shape
    tm = _row_tile(m, ROW_TILE_NORM)
    return _call(
        _prep_kernel, grid=(m // tm,),
        in_specs=[pl.BlockSpec((tm, d), lambda i: (i, 0)), pl.BlockSpec((1, d), lambda i: (0, 0))],
        out_specs=[pl.BlockSpec((tm, d), lambda i: (i, 0)), pl.BlockSpec((tm, 1), lambda i: (i, 0))],
        out_shape=[jax.ShapeDtypeStruct((m, d), BF16), jax.ShapeDtypeStruct((m, 1), F32)],
        scratch=(), vmem=5 * tm * d * 4 + 4 * MIB, name="norm_inputs", args=(x, g.reshape(1, d)),
        casts=casts)


def _rmsnorm_kernel(x_ref, g_ref, o_ref):
    x = x_ref[...]
    y = x * lax.rsqrt(jnp.mean(x * x, axis=-1, keepdims=True) + EPS)
    o_ref[...] = (y * g_ref[...]).astype(o_ref.dtype)


def _rmsnorm(x, g):
    m, d = x.shape
    tm = _row_tile(m, ROW_TILE_NORM)
    (y,), _ = _call(
        _rmsnorm_kernel, grid=(m // tm,),
        in_specs=[pl.BlockSpec((tm, d), lambda i: (i, 0)), pl.BlockSpec((1, d), lambda i: (0, 0))],
        out_specs=[pl.BlockSpec((tm, d), lambda i: (i, 0))],
        out_shape=[jax.ShapeDtypeStruct((m, d), F32)],
        scratch=(), vmem=7 * tm * d * 4 + 4 * MIB, name="rmsnorm", args=(x, g.reshape(1, d)))
    return y


def _retention_kernel(logg_ref, hg_ref, ssq_ref, w_ref, cos_ref, sin_ref, s0_ref, *rest,
                      tm, hd, chunk, emit_state):
    if emit_state:
        a_ref, sout_ref, state_ref, q_s, k_s, v_s, g_s = rest
    else:
        a_ref, state_ref, q_s, k_s, v_s, g_s = rest
    rb = pl.program_id(1)
    h = pl.program_id(2)
    half = hd // 2

    @pl.when(rb == 0)
    def _():
        state_ref[h] = s0_ref[...]

    lg = logg_ref[h]
    ri = lax.broadcasted_iota(jnp.int32, (chunk, chunk), 0)
    ci = lax.broadcasted_iota(jnp.int32, (chunk, chunk), 1)
    rel = (ri - ci).astype(F32)
    inner_decay = jnp.where(rel >= 0, jnp.exp(lg * jnp.maximum(rel, 0.0)), 0.0)
    jj = lax.broadcasted_iota(jnp.int32, (chunk, 1), 0).astype(F32)
    q_decay = jnp.exp(lg * (jj + 1.0))
    k_decay = jnp.exp(lg * (chunk - 1.0 - jj))
    chunk_decay = jnp.exp(jnp.full((1, hd), lg * chunk, F32))

    def project(rows):
        proj = _dot(hg_ref[rows, :], w_ref[...]) * _inv_rms(ssq_ref[rows, :], hg_ref.shape[1])
        cos = cos_ref[rows, :]
        sin = sin_ref[rows, :]

        def rot(t):
            t1, t2 = t[:, :half], t[:, half:]
            return jnp.concatenate([t1 * cos - t2 * sin, t1 * sin + t2 * cos], axis=-1)

        q_s[rows, :] = rot(proj[:, 0 * hd:1 * hd]).astype(BF16)
        k_s[rows, :] = rot(proj[:, 1 * hd:2 * hd]) * (hd ** -0.5)
        v_s[rows, :] = proj[:, 2 * hd:3 * hd].astype(BF16)
        g = proj[:, 3 * hd:4 * hd]
        g_s[rows, :] = g * jax.nn.sigmoid(g)

    def recur(c):
        rows = pl.ds(c * chunk, chunk)
        qc = q_s[rows, :]
        kc = k_s[rows, :]
        vc = v_s[rows, :]
        st = state_ref[h]
        scores = lax.dot_general(qc, kc.astype(BF16), (((1,), (1,)), ((), ())),
                                 preferred_element_type=F32) * inner_decay
        inner = _dot(scores.astype(BF16), vc)
        cross = _dot(qc, st.astype(BF16)) * q_decay
        state_ref[h] = st * chunk_decay + lax.dot_general(
            (kc * k_decay).astype(BF16), vc, (((0,), (0,)), ((), ())),
            preferred_element_type=F32)
        o = inner + cross
        o = o * lax.rsqrt(jnp.mean(o * o, axis=-1, keepdims=True) + EPS)
        a_ref[rows, :] = (o * g_s[rows, :]).astype(a_ref.dtype)

    gm = tm // ROW_GROUPS_RET if tm % (ROW_GROUPS_RET * chunk) == 0 else tm
    for r0 in range(0, tm, gm):
        project(pl.ds(r0, gm))
        for c in range(r0 // chunk, (r0 + gm) // chunk):
            recur(c)

    if emit_state:
        sout_ref[...] = state_ref[h]


def _retention(hg, ssq, w_ret, cos, sin, state0, log_gamma, *, batch, emit_state, casts=()):
    m, d = hg.shape
    lb = m // batch
    heads = RET_HEADS
    hd = w_ret.shape[1] // (4 * heads)
    tm = _row_tile(lb, ROW_TILE_RET)
    nrb = lb // tm
    chunk = RET_CHUNK if tm % RET_CHUNK == 0 else CHUNK
    out_shape = [jax.ShapeDtypeStruct((m, heads * hd), BF16)]
    out_specs = [pl.BlockSpec((tm, hd), lambda b, r, h: (b * nrb + r, h))]
    if emit_state:
        out_shape.append(jax.ShapeDtypeStruct((heads, hd, hd), F32))
        out_specs.append(pl.BlockSpec((None, hd, hd), lambda b, r, h: (h, 0, 0)))
    vmem = (2 * tm * d * 2 + 2 * d * 4 * hd * 2 + 3 * tm * 4 * hd * 4 // ROW_GROUPS_RET
            + heads * hd * hd * 4 + 6 * MIB)
    return _call(
        functools.partial(_retention_kernel, tm=tm, hd=hd, chunk=chunk, emit_state=emit_state),
        grid=(batch, nrb, heads),
        in_specs=[
            pl.BlockSpec(memory_space=pltpu.SMEM),
            pl.BlockSpec((tm, d), lambda b, r, h: (b * nrb + r, 0)),
            pl.BlockSpec((tm, 1), lambda b, r, h: (b * nrb + r, 0)),
            pl.BlockSpec((d, 4 * hd), lambda b, r, h: (0, h)),
            pl.BlockSpec((tm, hd // 2), lambda b, r, h: (r, 0)),
            pl.BlockSpec((tm, hd // 2), lambda b, r, h: (r, 0)),
            pl.BlockSpec((None, hd, hd), lambda b, r, h: (h, 0, 0)),
        ],
        out_specs=out_specs, out_shape=out_shape,
        scratch=[
            pltpu.VMEM((heads, hd, hd), F32),
            pltpu.VMEM((tm, hd), BF16),
            pltpu.VMEM((tm, hd), F32),
            pltpu.VMEM((tm, hd), BF16),
            pltpu.VMEM((tm, hd), F32),
        ],
        vmem=vmem, name="retention",
        args=(log_gamma, hg, ssq, w_ret, cos, sin, state0), casts=casts)


def _conv_kernel(hg_ref, ssq_ref, w_ref, cw_ref, halo_ref, *rest, tm, tn, nrb, emit_tail):
    if emit_tail:
        b1_ref, tail_ref, carry_ref = rest
    else:
        b1_ref, carry_ref = rest
    i = pl.program_id(0)
    j = pl.program_id(1)

    @pl.when(i % nrb == 0)
    def _():
        carry_ref[j] = halo_ref[...]

    proj = _dot(hg_ref[...], w_ref[...]) * _inv_rms(ssq_ref[...], hg_ref.shape[1])
    zc = proj[:, :tn] * proj[:, 2 * tn:]
    prev = carry_ref[j]
    p1 = prev[SUBLANES - 1:SUBLANES, :]
    p2 = prev[SUBLANES - 2:SUBLANES - 1, :]
    row = lax.broadcasted_iota(jnp.int32, zc.shape, 0)
    z1 = jnp.where(row == 0, p1, pltpu.roll(zc, 1, 0))
    z2 = jnp.where(row == 0, p2, jnp.where(row == 1, p1, pltpu.roll(zc, 2, 0)))
    cw = cw_ref[...]
    conv = cw[0:1, :] * z2 + cw[1:2, :] * z1 + cw[2:3, :] * zc
    b1_ref[...] = (proj[:, tn:2 * tn] * conv).astype(b1_ref.dtype)
    tail = zc[tm - SUBLANES:, :]
    carry_ref[j] = tail
    if emit_tail:
        tail_ref[...] = tail


def _conv_mixer(hg, ssq, w_cbz, conv_w, halo, *, batch, emit_tail, casts=()):
    m, d = hg.shape
    lb = m // batch
    dc = conv_w.shape[1]
    tn = _conv_tile(dc)
    tm = _row_tile(lb, ROW_TILE_CONV)
    nrb = lb // tm
    ni, nj = m // tm, dc // tn
    out_shape = [jax.ShapeDtypeStruct((m, dc), BF16)]
    out_specs = [pl.BlockSpec((tm, tn), lambda i, j: (i, j))]
    if emit_tail:
        out_shape.append(jax.ShapeDtypeStruct((ni * SUBLANES, dc), F32))
        out_specs.append(pl.BlockSpec((SUBLANES, tn), lambda i, j: (i, j)))
    vmem = 2 * tm * d * 2 + 2 * d * 3 * tn * 2 + 3 * tm * 3 * tn * 4 + 4 * MIB
    return _call(
        functools.partial(_conv_kernel, tm=tm, tn=tn, nrb=nrb, emit_tail=emit_tail),
        grid=(ni, nj),
        in_specs=[
            pl.BlockSpec((tm, d), lambda i, j: (i, 0)),
            pl.BlockSpec((tm, 1), lambda i, j: (i, 0)),
            pl.BlockSpec((d, 3 * tn), lambda i, j: (0, j)),
            pl.BlockSpec((CONV_WIDTH, tn), lambda i, j: (0, j)),
            pl.BlockSpec((SUBLANES, tn), lambda i, j: (0, j)),
        ],
        out_specs=out_specs, out_shape=out_shape,
        scratch=[pltpu.VMEM((nj, SUBLANES, tn), F32)],
        vmem=vmem, name="conv_mixer", args=(hg, ssq, w_cbz, conv_w, halo), casts=casts)


def _merge_kernel(a_ref, b_ref, hg_ref, ssq_ref, wro_ref, wco_ref, wg_ref, o_ref, *, tn):
    ya = _dot(a_ref[...], wro_ref[...])
    yb = _dot(b_ref[...], wco_ref[...])
    gates = _dot(hg_ref[...], wg_ref[...]) * _inv_rms(ssq_ref[...], hg_ref.shape[1])
    o_ref[...] = (jax.nn.sigmoid(gates[:, :tn]) * ya
                  + jax.nn.sigmoid(gates[:, tn:]) * yb).astype(o_ref.dtype)


def _merge(a1, b1, hg, ssq, w_ro, w_co, w_gate, *, casts=()):
    m, d = hg.shape
    dr = a1.shape[1]
    dc = b1.shape[1]
    tn = _merge_tile(d)
    tm = _row_tile(m, ROW_TILE_MERGE)
    vmem = (2 * tm * (dr + dc + d) * 2 + 2 * (dr + dc + 2 * d) * tn * 2
            + 8 * tm * tn * 4 + 2 * MIB)
    (merged,), cast_out = _call(
        functools.partial(_merge_kernel, tn=tn),
        grid=(m // tm, d // tn),
        in_specs=[
            pl.BlockSpec((tm, dr), lambda i, j: (i, 0)),
            pl.BlockSpec((tm, dc), lambda i, j: (i, 0)),
            pl.BlockSpec((tm, d), lambda i, j: (i, 0)),
            pl.BlockSpec((tm, 1), lambda i, j: (i, 0)),
            pl.BlockSpec((dr, tn), lambda i, j: (0, j)),
            pl.BlockSpec((dc, tn), lambda i, j: (0, j)),
            pl.BlockSpec((d, 2 * tn), lambda i, j: (0, j)),
        ],
        out_specs=[pl.BlockSpec((tm, tn), lambda i, j: (i, j))],
        out_shape=[jax.ShapeDtypeStruct((m, d), BF16)],
        scratch=(), vmem=vmem, name="merge",
        args=(a1, b1, hg, ssq, w_ro, w_co, w_gate), casts=casts)
    return merged, cast_out


def _residual_kernel(a_ref, w_ref, h_ref, *rest, nt, emit_scaled):
    if emit_scaled:
        g_ref, o_ref, hg_ref, ssq_ref, acc_ref = rest
    else:
        (o_ref,) = rest
    j = pl.program_id(1)
    if emit_scaled:
        @pl.when(j == 0)
        def _():
            acc_ref[...] = jnp.zeros_like(acc_ref)

    hn = h_ref[...] + _dot(a_ref[...], w_ref[...])
    o_ref[...] = hn
    if emit_scaled:
        hg_ref[...] = (hn * g_ref[...]).astype(hg_ref.dtype)
        sq = hn * hn
        part = sq[:, :LANES]
        for c in range(1, sq.shape[1] // LANES):
            part = part + sq[:, c * LANES:(c + 1) * LANES]
        acc_ref[...] += part

        @pl.when(j == nt - 1)
        def _():
            ssq_ref[...] = jnp.sum(acc_ref[...], axis=-1, keepdims=True)


def _residual(a, w_tiles, h, g_next, *, k_splits, casts=()):
    m, kdim = a.shape
    nt, _, tn = w_tiles.shape
    d = nt * tn
    kb = kdim // k_splits
    tm = _row_tile(m, ROW_TILE_RES)
    casts = casts or ((),) * k_splits
    hg = ssq = None
    cast_out = []
    for s in range(k_splits):
        emit_scaled = g_next is not None and s == k_splits - 1
        in_specs = [
            pl.BlockSpec((tm, kb), lambda i, j, s=s: (i, s)),
            pl.BlockSpec((None, kb, tn), lambda i, j, s=s: (j, s, 0)),
            pl.BlockSpec((tm, tn), lambda i, j: (i, j)),
        ]
        out_specs = [pl.BlockSpec((tm, tn), lambda i, j: (i, j))]
        out_shape = [jax.ShapeDtypeStruct((m, d), F32)]
        args = [a, w_tiles, h]
        scratch = []
        vmem = 2 * tm * kb * 2 + 2 * kb * tn * 2 + 5 * tm * tn * 4 + 4 * MIB
        if emit_scaled:
            in_specs.append(pl.BlockSpec((1, tn), lambda i, j: (0, j)))
            args.append(g_next.reshape(1, d))
            out_specs += [pl.BlockSpec((tm, tn), lambda i, j: (i, j)),
                          pl.BlockSpec((tm, 1), lambda i, j: (i, 0))]
            out_shape += [jax.ShapeDtypeStruct((m, d), BF16), jax.ShapeDtypeStruct((m, 1), F32)]
            scratch.append(pltpu.VMEM((tm, LANES), F32))
            vmem += 2 * tm * tn * 2 + 4 * tm * LANES * 4
        outs, cw = _call(
            functools.partial(_residual_kernel, nt=nt, emit_scaled=emit_scaled),
            grid=(m // tm, nt), in_specs=in_specs, out_specs=out_specs, out_shape=out_shape,
            scratch=scratch, vmem=vmem, name="residual", args=args, casts=casts[s])
        cast_out += cw
        h = outs[0]
        if emit_scaled:
            hg, ssq = outs[1], outs[2]
    return (h, hg, ssq), cast_out


def _up_kernel(hg_ref, ssq_ref, w_ref, o_ref):
    t = jnp.maximum(_dot(hg_ref[...], w_ref[...]) * _inv_rms(ssq_ref[...], hg_ref.shape[1]), 0.0)
    o_ref[...] = (t * t).astype(o_ref.dtype)


def _up(hg, ssq, w_up, *, casts=()):
    m, d = hg.shape
    f = w_up.shape[1]
    tm = _row_tile(m, ROW_TILE_UP)
    tn = min(f, COL_TILE_UP)
    vmem = 2 * tm * d * 2 + 2 * d * tn * 2 + 2 * tm * tn * 2 + 3 * tm * tn * 4 + 2 * MIB
    (hid,), cast_out = _call(
        _up_kernel, grid=(m // tm, f // tn),
        in_specs=[pl.BlockSpec((tm, d), lambda i, j: (i, 0)),
                  pl.BlockSpec((tm, 1), lambda i, j: (i, 0)),
                  pl.BlockSpec((d, tn), lambda i, j: (0, j))],
        out_specs=[pl.BlockSpec((tm, tn), lambda i, j: (i, j))],
        out_shape=[jax.ShapeDtypeStruct((m, f), BF16)],
        scratch=(), vmem=vmem, name="mlp_up", args=(hg, ssq, w_up), casts=casts)
    return hid, cast_out


def _rope_tables(pos, hd):
    inv_freq = ROPE_BASE ** (-jnp.arange(0, hd, 2, dtype=F32) / hd)
    ang = pos[:, None] * inv_freq[None, :]
    return jnp.cos(ang), jnp.sin(ang)


def kernel(x, meta_tokens, norm_mix_g, w_in, conv_w, w_ret_out, w_conv_out, w_out,
           norm_mlp_g, w_up, w_down, final_norm_g):
    batch, seq, d = x.shape
    depth = w_in.shape[0]
    n_meta = meta_tokens.shape[0]
    dr = w_ret_out.shape[1]
    dc = w_conv_out.shape[1]
    f = w_up.shape[2]
    hd = dr // RET_HEADS
    assert seq % CHUNK == 0 and n_meta <= CHUNK
    meta_pad = CHUNK - n_meta

    def cast(layer, name):
        spec = {
            "w_ret": (w_in, 0, 4, dr, hd, False),
            "w_cbz": (w_in, 4 * dr, 3, dc, _conv_tile(dc), False),
            "w_gate": (w_in, 4 * dr + 3 * dc, 2, d, _merge_tile(d), False),
            "w_ro": (w_ret_out, 0, 1, d, _merge_tile(d), False),
            "w_co": (w_conv_out, 0, 1, d, _merge_tile(d), False),
            "w_o": (w_out, 0, 1, d, min(d, COL_TILE_RES_O), True),
            "w_up": (w_up, 0, 1, f, min(f, COL_TILE_UP), False),
            "w_down": (w_down, 0, 1, d, min(d, COL_TILE_RES_DOWN), True),
        }[name]
        return _Cast(spec[0], layer, *spec[1:])

    lw = [dict() for _ in range(depth)]

    def riding(targets):
        targets = [(layer, name) for layer, name in targets if layer < depth]
        return targets, tuple(cast(layer, name) for layer, name in targets)

    def keep(targets, cast_out):
        for (layer, name), w in zip(targets, cast_out):
            lw[layer][name] = w

    log_gamma = jnp.asarray(
        np.log1p(-np.exp2(-5.0 - np.arange(RET_HEADS, dtype=np.float64))), F32)
    cos_x, sin_x = _rope_tables(jnp.arange(seq, dtype=F32) + n_meta, hd)
    cos_m, sin_m = _rope_tables(jnp.arange(CHUNK, dtype=F32) - meta_pad, hd)

    hm = jnp.concatenate([jnp.zeros((meta_pad, d), F32), meta_tokens.astype(F32)], axis=0)
    meta = dict(h=hm, batch=1, cos=cos_m, sin=sin_m, state=jnp.zeros((RET_HEADS, hd, hd), F32),
                halo=jnp.zeros((SUBLANES, dc), F32), is_meta=True)
    real = dict(h=x.reshape(batch * seq, d), batch=batch, cos=cos_x, sin=sin_x, is_meta=False)
    for st in (meta, real):
        targets, casts = riding([] if st["is_meta"] else [(0, "w_ret")])
        (st["hg"], st["ssq"]), out = _prep(st["h"], norm_mix_g[0], casts=casts)
        keep(targets, out)

    def stage(streams, fn, *target_groups):
        groups = [riding(t) for t in target_groups]
        for st in streams:
            casts = [() if st["is_meta"] else c for _, c in groups]
            updates, out = fn(st, *casts)
            st.update(updates)
            if not st["is_meta"]:
                keep([t for targets, _ in groups for t in targets], out)

    for i in range(depth):
        w = lw[i]
        nxt = i + 1
        streams = (meta, real)

        def ret(st, casts):
            outs, out = _retention(st["hg"], st["ssq"], w["w_ret"], st["cos"], st["sin"],
                                   st["state"], log_gamma, batch=st["batch"],
                                   emit_state=st["is_meta"], casts=casts)
            if st["is_meta"]:
                real["state"] = outs[1]
            return dict(a1=outs[0]), out

        def conv(st, casts):
            outs, out = _conv_mixer(st["hg"], st["ssq"], w["w_cbz"], conv_w[i], st["halo"],
                                    batch=st["batch"], emit_tail=st["is_meta"], casts=casts)
            if st["is_meta"]:
                real["halo"] = outs[1]
            return dict(b1=outs[0]), out

        def merge(st, casts):
            merged, out = _merge(st["a1"], st["b1"], st["hg"], st["ssq"], w["w_ro"], w["w_co"],
                                 w["w_gate"], casts=casts)
            return dict(merged=merged), out

        def project_out(st, casts):
            (h, hg, ssq), out = _residual(st["merged"], w["w_o"], st["h"], norm_mlp_g[i],
                                          k_splits=1, casts=(casts,))
            return dict(h=h, hg=hg, ssq=ssq), out

        def up(st, casts):
            hid, out = _up(st["hg"], st["ssq"], w["w_up"], casts=casts)
            return dict(hid=hid), out

        def down(st, *casts):
            g_next = norm_mix_g[nxt] if nxt < depth else None
            (h, hg, ssq), out = _residual(st["hid"], w["w_down"], st["h"], g_next,
                                          k_splits=K_SPLITS_DOWN, casts=casts)
            return dict(h=h, hg=hg, ssq=ssq), out

        def own(*names):
            return [(i, n) for n in names]

        stage(streams, ret, own("w_cbz", "w_up"))
        stage(streams, conv, own("w_gate", "w_ro", "w_co", "w_o", "w_down") + [(nxt, "w_ret")])
        if nxt == depth:
            streams = (real,)
        stage(streams, merge, [])
        stage(streams, project_out, [])
        stage(streams, up, [])
        stage(streams, down, *[[] for _ in range(K_SPLITS_DOWN)])
    return _rmsnorm(real["h"], final_norm_g).reshape(batch, seq, d)
```

```python
import functools
from typing import NamedTuple

import numpy as np
import jax
import jax.numpy as jnp
from jax import lax
from jax.experimental import pallas as pl
from jax.experimental.pallas import tpu as pltpu

RET_HEADS = 8
CONV_WIDTH = 3
CHUNK = 128
ROPE_BASE = 10000.0
EPS = 1e-6

MIB = 1024 * 1024
V7X_VMEM_BYTES = 64 * MIB
VMEM_CAP_BYTES = V7X_VMEM_BYTES - 4 * MIB
SUBLANES = 8
LANES = 128

F32 = jnp.float32
BF16 = jnp.bfloat16

ROW_TILE_NORM = 512
ROW_TILE_RET = 1024
ROW_GROUPS_RET = 2
RET_CHUNK = 256
ROW_TILE_CONV = 1024
COL_TILE_CONV = 256
ROW_TILE_MERGE = 512
COL_TILE_MERGE = 512
ROW_TILE_RES = 1024
COL_TILE_RES_O = 1024
COL_TILE_RES_DOWN = 256
K_SPLITS_DOWN = 2
ROW_TILE_UP = 1024
COL_TILE_UP = 1024
MIN_CAST_BLOCK_ELEMS = 64 * 1024


def _dot(a, b):
    return jnp.dot(a, b, preferred_element_type=F32)


def _row_tile(m, want):
    tm = min(m, want)
    assert m % tm == 0, (m, tm)
    return tm


def _conv_tile(dc):
    return min(dc, COL_TILE_CONV)


def _merge_tile(d):
    return min(d, COL_TILE_MERGE)


def _inv_rms(ssq, d):
    return lax.rsqrt(ssq / d + EPS)


class _Cast(NamedTuple):
    w: jax.Array
    layer: int
    col0: int
    sections: int
    width: int
    tile: int
    tile_major: bool
    block_elems: int = 0


def _cast_geometry(c):
    kdim = c.w.shape[1]
    tiles_per_section = c.width // c.tile
    n_tiles = c.sections * tiles_per_section
    tk = min(kdim, c.block_elems // c.tile)
    return kdim, tiles_per_section, n_tiles, tk, kdim // tk


def _fit_blocks(c, n_steps):
    c = c._replace(block_elems=MIN_CAST_BLOCK_ELEMS)
    while _cast_geometry(c)[2] * _cast_geometry(c)[4] > n_steps:
        assert _cast_geometry(c)[3] < c.w.shape[1], (c.block_elems, n_steps)
        c = c._replace(block_elems=2 * c.block_elems)
    return c


def _cast_specs(c, step_of):
    kdim, tiles_per_section, n_tiles, tk, kblocks = _cast_geometry(c)
    n_blocks = n_tiles * kblocks
    base = c.col0 // c.tile

    def tile_and_kblock(*idx):
        t = jnp.minimum(step_of(*idx), n_blocks - 1)
        return t // kblocks, t % kblocks

    def src(*idx):
        p, kk = tile_and_kblock(*idx)
        return c.layer, kk, base + (p % c.sections) * tiles_per_section + p // c.sections

    if c.tile_major:
        shape = (n_tiles, kdim, c.tile)
        block = (None, tk, c.tile)

        def dst(*idx):
            p, kk = tile_and_kblock(*idx)
            return p, kk, 0
    else:
        shape = (kdim, n_tiles * c.tile)
        block = (tk, c.tile)

        def dst(*idx):
            p, kk = tile_and_kblock(*idx)
            return kk, p

    return (pl.BlockSpec((None, tk, c.tile), src), pl.BlockSpec(block, dst),
            jax.ShapeDtypeStruct(shape, BF16), n_blocks)


def _call(body, *, grid, in_specs, out_specs, out_shape, scratch, vmem, name, args, casts=()):
    n_in, n_out, n_cast = len(in_specs), len(out_specs), len(casts)

    def step_of(*idx):
        t = idx[0]
        for size, i in zip(grid[1:], idx[1:]):
            t = t * size + i
        return t

    n_steps = int(np.prod(grid))
    casts = [c if c.block_elems else _fit_blocks(c, n_steps) for c in casts]
    specs = [_cast_specs(c, step_of) for c in casts]
    for c, s in zip(casts, specs):
        assert s[3] <= n_steps, (name, s[3], n_steps)
        vmem += 2 * c.block_elems * (4 + 2)

    def kernel(*refs):
        ins = refs[:n_in]
        cast_in = refs[n_in:n_in + n_cast]
        outs = refs[n_in + n_cast:n_in + n_cast + n_out]
        cast_out = refs[n_in + n_cast + n_out:n_in + 2 * n_cast + n_out]
        for x_ref, o_ref in zip(cast_in, cast_out):
            o_ref[...] = x_ref[...].astype(o_ref.dtype)
        body(*ins, *outs, *refs[n_in + 2 * n_cast + n_out:])

    assert vmem <= VMEM_CAP_BYTES, (name, vmem)
    res = pl.pallas_call(
        kernel,
        grid=grid,
        in_specs=[*in_specs, *[s[0] for s in specs]],
        out_specs=[*out_specs, *[s[1] for s in specs]],
        out_shape=[*out_shape, *[s[2] for s in specs]],
        scratch_shapes=list(scratch),
        compiler_params=pltpu.CompilerParams(
            dimension_semantics=("arbitrary",) * len(grid), vmem_limit_bytes=int(vmem)),
        name=name,
    )(*args, *[c.w for c in casts])
    return list(res[:n_out]), list(res[n_out:])


def _prep_kernel(x_ref, g_ref, hg_ref, ssq_ref):
    x = x_ref[...]
    hg_ref[...] = (x * g_ref[...]).astype(hg_ref.dtype)
    ssq_ref[...] = jnp.sum(x * x, axis=-1, keepdims=True)


def _prep(x, g, *, casts=()):
    m, d = x.shape
    tm = _row_tile(m, ROW_TILE_NORM)
    return _call(
        _prep_kernel, grid=(m // tm,),
        in_specs=[pl.BlockSpec((tm, d), lambda i: (i, 0)), pl.BlockSpec((1, d), lambda i: (0, 0))],
        out_specs=[pl.BlockSpec((tm, d), lambda i: (i, 0)), pl.BlockSpec((tm, 1), lambda i: (i, 0))],
        out_shape=[jax.ShapeDtypeStruct((m, d), BF16), jax.ShapeDtypeStruct((m, 1), F32)],
        scratch=(), vmem=5 * tm * d * 4 + 4 * MIB, name="norm_inputs", args=(x, g.reshape(1, d)),
        casts=casts)


def _rmsnorm_kernel(x_ref, g_ref, o_ref):
    x = x_ref[...]
    y = x * lax.rsqrt(jnp.mean(x * x, axis=-1, keepdims=True) + EPS)
    o_ref[...] = (y * g_ref[...]).astype(o_ref.dtype)


def _rmsnorm(x, g):
    m, d = x.shape
    tm = _row_tile(m, ROW_TILE_NORM)
    (y,), _ = _call(
        _rmsnorm_kernel, grid=(m // tm,),
        in_specs=[pl.BlockSpec((tm, d), lambda i: (i, 0)), pl.BlockSpec((1, d), lambda i: (0, 0))],
        out_specs=[pl.BlockSpec((tm, d), lambda i: (i, 0))],
        out_shape=[jax.ShapeDtypeStruct((m, d), F32)],
        scratch=(), vmem=7 * tm * d * 4 + 4 * MIB, name="rmsnorm", args=(x, g.reshape(1, d)))
    return y


def _retention_kernel(logg_ref, hg_ref, ssq_ref, w_ref, cos_ref, sin_ref, s0_ref, *rest,
                      tm, hd, chunk, emit_state):
    if emit_state:
        a_ref, sout_ref, state_ref, q_s, k_s, v_s, g_s = rest
    else:
        a_ref, state_ref, q_s, k_s, v_s, g_s = rest
    rb = pl.program_id(1)
    h = pl.program_id(2)
    half = hd // 2

    @pl.when(rb == 0)
    def _():
        state_ref[h] = s0_ref[...]

    lg = logg_ref[h]
    ri = lax.broadcasted_iota(jnp.int32, (chunk, chunk), 0)
    ci = lax.broadcasted_iota(jnp.int32, (chunk, chunk), 1)
    rel = (ri - ci).astype(F32)
    inner_decay = jnp.where(rel >= 0, jnp.exp(lg * jnp.maximum(rel, 0.0)), 0.0)
    jj = lax.broadcasted_iota(jnp.int32, (chunk, 1), 0).astype(F32)
    q_decay = jnp.exp(lg * (jj + 1.0))
    k_decay = jnp.exp(lg * (chunk - 1.0 - jj))
    chunk_decay = jnp.exp(jnp.full((1, hd), lg * chunk, F32))

    def project(rows):
        proj = _dot(hg_ref[rows, :], w_ref[...]) * _inv_rms(ssq_ref[rows, :], hg_ref.shape[1])
        cos = cos_ref[rows, :]
        sin = sin_ref[rows, :]

        def rot(t):
            t1, t2 = t[:, :half], t[:, half:]
            return jnp.concatenate([t1 * cos - t2 * sin, t1 * sin + t2 * cos], axis=-1)

        q_s[rows, :] = rot(proj[:, 0 * hd:1 * hd]).astype(BF16)
        k_s[rows, :] = rot(proj[:, 1 * hd:2 * hd]) * (hd ** -0.5)
        v_s[rows, :] = proj[:, 2 * hd:3 * hd].astype(BF16)
        g = proj[:, 3 * hd:4 * hd]
        g_s[rows, :] = g * jax.nn.sigmoid(g)

    def recur(c):
        rows = pl.ds(c * chunk, chunk)
        qc = q_s[rows, :]
        kc = k_s[rows, :]
        vc = v_s[rows, :]
        st = state_ref[h]
        scores = lax.dot_general(qc, kc.astype(BF16), (((1,), (1,)), ((), ())),
                                 preferred_element_type=F32) * inner_decay
        inner = _dot(scores.astype(BF16), vc)
        cross = _dot(qc, st.astype(BF16)) * q_decay
        state_ref[h] = st * chunk_decay + lax.dot_general(
            (kc * k_decay).astype(BF16), vc, (((0,), (0,)), ((), ())),
            preferred_element_type=F32)
        o = inner + cross
        o = o * lax.rsqrt(jnp.mean(o * o, axis=-1, keepdims=True) + EPS)
        a_ref[rows, :] = (o * g_s[rows, :]).astype(a_ref.dtype)

    gm = tm // ROW_GROUPS_RET if tm % (ROW_GROUPS_RET * chunk) == 0 else tm
    for r0 in range(0, tm, gm):
        project(pl.ds(r0, gm))
        for c in range(r0 // chunk, (r0 + gm) // chunk):
            recur(c)

    if emit_state:
        sout_ref[...] = state_ref[h]


def _retention(hg, ssq, w_ret, cos, sin, state0, log_gamma, *, batch, emit_state, casts=()):
    m, d = hg.shape
    lb = m // batch
    heads = RET_HEADS
    hd = w_ret.shape[1] // (4 * heads)
    tm = _row_tile(lb, ROW_TILE_RET)
    nrb = lb // tm
    chunk = RET_CHUNK if tm % RET_CHUNK == 0 else CHUNK
    out_shape = [jax.ShapeDtypeStruct((m, heads * hd), BF16)]
    out_specs = [pl.BlockSpec((tm, hd), lambda b, r, h: (b * nrb + r, h))]
    if emit_state:
        out_shape.append(jax.ShapeDtypeStruct((heads, hd, hd), F32))
        out_specs.append(pl.BlockSpec((None, hd, hd), lambda b, r, h: (h, 0, 0)))
    vmem = (2 * tm * d * 2 + 2 * d * 4 * hd * 2 + 3 * tm * 4 * hd * 4 // ROW_GROUPS_RET
            + heads * hd * hd * 4 + 6 * MIB)
    return _call(
        functools.partial(_retention_kernel, tm=tm, hd=hd, chunk=chunk, emit_state=emit_state),
        grid=(batch, nrb, heads),
        in_specs=[
            pl.BlockSpec(memory_space=pltpu.SMEM),
            pl.BlockSpec((tm, d), lambda b, r, h: (b * nrb + r, 0)),
            pl.BlockSpec((tm, 1), lambda b, r, h: (b * nrb + r, 0)),
            pl.BlockSpec((d, 4 * hd), lambda b, r, h: (0, h)),
            pl.BlockSpec((tm, hd // 2), lambda b, r, h: (r, 0)),
            pl.BlockSpec((tm, hd // 2), lambda b, r, h: (r, 0)),
            pl.BlockSpec((None, hd, hd), lambda b, r, h: (h, 0, 0)),
        ],
        out_specs=out_specs, out_shape=out_shape,
        scratch=[
            pltpu.VMEM((heads, hd, hd), F32),
            pltpu.VMEM((tm, hd), BF16),
            pltpu.VMEM((tm, hd), F32),
            pltpu.VMEM((tm, hd), BF16),
            pltpu.VMEM((tm, hd), F32),
        ],
        vmem=vmem, name="retention",
        args=(log_gamma, hg, ssq, w_ret, cos, sin, state0), casts=casts)


def _conv_kernel(hg_ref, ssq_ref, w_ref, cw_ref, halo_ref, *rest, tm, tn, nrb, emit_tail):
    if emit_tail:
        b1_ref, tail_ref, carry_ref = rest
    else:
        b1_ref, carry_ref = rest
    i = pl.program_id(0)
    j = pl.program_id(1)

    @pl.when(i % nrb == 0)
    def _():
        carry_ref[j] = halo_ref[...]

    proj = _dot(hg_ref[...], w_ref[...]) * _inv_rms(ssq_ref[...], hg_ref.shape[1])
    zc = proj[:, :tn] * proj[:, 2 * tn:]
    prev = carry_ref[j]
    p1 = prev[SUBLANES - 1:SUBLANES, :]
    p2 = prev[SUBLANES - 2:SUBLANES - 1, :]
    row = lax.broadcasted_iota(jnp.int32, zc.shape, 0)
    z1 = jnp.where(row == 0, p1, pltpu.roll(zc, 1, 0))
    z2 = jnp.where(row == 0, p2, jnp.where(row == 1, p1, pltpu.roll(zc, 2, 0)))
    cw = cw_ref[...]
    conv = cw[0:1, :] * z2 + cw[1:2, :] * z1 + cw[2:3, :] * zc
    b1_ref[...] = (proj[:, tn:2 * tn] * conv).astype(b1_ref.dtype)
    tail = zc[tm - SUBLANES:, :]
    carry_ref[j] = tail
    if emit_tail:
        tail_ref[...] = tail


def _conv_mixer(hg, ssq, w_cbz, conv_w, halo, *, batch, emit_tail, casts=()):
    m, d = hg.shape
    lb = m // batch
    dc = conv_w.shape[1]
    tn = _conv_tile(dc)
    tm = _row_tile(lb, ROW_TILE_CONV)
    nrb = lb // tm
    ni, nj = m // tm, dc // tn
    out_shape = [jax.ShapeDtypeStruct((m, dc), BF16)]
    out_specs = [pl.BlockSpec((tm, tn), lambda i, j: (i, j))]
    if emit_tail:
        out_shape.append(jax.ShapeDtypeStruct((ni * SUBLANES, dc), F32))
        out_specs.append(pl.BlockSpec((SUBLANES, tn), lambda i, j: (i, j)))
    vmem = 2 * tm * d * 2 + 2 * d * 3 * tn * 2 + 3 * tm * 3 * tn * 4 + 4 * MIB
    return _call(
        functools.partial(_conv_kernel, tm=tm, tn=tn, nrb=nrb, emit_tail=emit_tail),
        grid=(ni, nj),
        in_specs=[
            pl.BlockSpec((tm, d), lambda i, j: (i, 0)),
            pl.BlockSpec((tm, 1), lambda i, j: (i, 0)),
            pl.BlockSpec((d, 3 * tn), lambda i, j: (0, j)),
            pl.BlockSpec((CONV_WIDTH, tn), lambda i, j: (0, j)),
            pl.BlockSpec((SUBLANES, tn), lambda i, j: (0, j)),
        ],
        out_specs=out_specs, out_shape=out_shape,
        scratch=[pltpu.VMEM((nj, SUBLANES, tn), F32)],
        vmem=vmem, name="conv_mixer", args=(hg, ssq, w_cbz, conv_w, halo), casts=casts)


def _merge_kernel(a_ref, b_ref, hg_ref, ssq_ref, wro_ref, wco_ref, wg_ref, o_ref, *, tn):
    ya = _dot(a_ref[...], wro_ref[...])
    yb = _dot(b_ref[...], wco_ref[...])
    gates = _dot(hg_ref[...], wg_ref[...]) * _inv_rms(ssq_ref[...], hg_ref.shape[1])
    o_ref[...] = (jax.nn.sigmoid(gates[:, :tn]) * ya
                  + jax.nn.sigmoid(gates[:, tn:]) * yb).astype(o_ref.dtype)


def _merge(a1, b1, hg, ssq, w_ro, w_co, w_gate, *, casts=()):
    m, d = hg.shape
    dr = a1.shape[1]
    dc = b1.shape[1]
    tn = _merge_tile(d)
    tm = _row_tile(m, ROW_TILE_MERGE)
    vmem = (2 * tm * (dr + dc + d) * 2 + 2 * (dr + dc + 2 * d) * tn * 2
            + 8 * tm * tn * 4 + 2 * MIB)
    (merged,), cast_out = _call(
        functools.partial(_merge_kernel, tn=tn),
        grid=(m // tm, d // tn),
        in_specs=[
            pl.BlockSpec((tm, dr), lambda i, j: (i, 0)),
            pl.BlockSpec((tm, dc), lambda i, j: (i, 0)),
            pl.BlockSpec((tm, d), lambda i, j: (i, 0)),
            pl.BlockSpec((tm, 1), lambda i, j: (i, 0)),
            pl.BlockSpec((dr, tn), lambda i, j: (0, j)),
            pl.BlockSpec((dc, tn), lambda i, j: (0, j)),
            pl.BlockSpec((d, 2 * tn), lambda i, j: (0, j)),
        ],
        out_specs=[pl.BlockSpec((tm, tn), lambda i, j: (i, j))],
        out_shape=[jax.ShapeDtypeStruct((m, d), BF16)],
        scratch=(), vmem=vmem, name="merge",
        args=(a1, b1, hg, ssq, w_ro, w_co, w_gate), casts=casts)
    return merged, cast_out


def _residual_kernel(a_ref, w_ref, h_ref, *rest, nt, emit_scaled):
    if emit_scaled:
        g_ref, o_ref, hg_ref, ssq_ref, acc_ref = rest
    else:
        (o_ref,) = rest
    j = pl.program_id(1)
    if emit_scaled:
        @pl.when(j == 0)
        def _():
            acc_ref[...] = jnp.zeros_like(acc_ref)

    hn = h_ref[...] + _dot(a_ref[...], w_ref[...])
    o_ref[...] = hn
    if emit_scaled:
        hg_ref[...] = (hn * g_ref[...]).astype(hg_ref.dtype)
        sq = hn * hn
        part = sq[:, :LANES]
        for c in range(1, sq.shape[1] // LANES):
            part = part + sq[:, c * LANES:(c + 1) * LANES]
        acc_ref[...] += part

        @pl.when(j == nt - 1)
        def _():
            ssq_ref[...] = jnp.sum(acc_ref[...], axis=-1, keepdims=True)


def _residual(a, w_tiles, h, g_next, *, k_splits, casts=()):
    m, kdim = a.shape
    nt, _, tn = w_tiles.shape
    d = nt * tn
    kb = kdim // k_splits
    tm = _row_tile(m, ROW_TILE_RES)
    casts = casts or ((),) * k_splits
    hg = ssq = None
    cast_out = []
    for s in range(k_splits):
        emit_scaled = g_next is not None and s == k_splits - 1
        in_specs = [
            pl.BlockSpec((tm, kb), lambda i, j, s=s: (i, s),
                         **({"pipeline_mode": pl.Buffered(1)} if k_splits == 1 else {})),
            pl.BlockSpec((None, kb, tn), lambda i, j, s=s: (j, s, 0)),
            pl.BlockSpec((tm, tn), lambda i, j: (i, j)),
        ]
        out_specs = [pl.BlockSpec((tm, tn), lambda i, j: (i, j))]
        out_shape = [jax.ShapeDtypeStruct((m, d), F32)]
        args = [a, w_tiles, h]
        scratch = []
        vmem = ((1 if k_splits == 1 else 2) * tm * kb * 2 + 2 * kb * tn * 2 + 5 * tm * tn * 4
                + 4 * MIB)
        if emit_scaled:
            in_specs.append(pl.BlockSpec((1, tn), lambda i, j: (0, j)))
            args.append(g_next.reshape(1, d))
            out_specs += [pl.BlockSpec((tm, tn), lambda i, j: (i, j)),
                          pl.BlockSpec((tm, 1), lambda i, j: (i, 0))]
            out_shape += [jax.ShapeDtypeStruct((m, d), BF16), jax.ShapeDtypeStruct((m, 1), F32)]
            scratch.append(pltpu.VMEM((tm, LANES), F32))
            vmem += 2 * tm * tn * 2 + 4 * tm * LANES * 4
        outs, cw = _call(
            functools.partial(_residual_kernel, nt=nt, emit_scaled=emit_scaled),
            grid=(m // tm, nt), in_specs=in_specs, out_specs=out_specs, out_shape=out_shape,
            scratch=scratch, vmem=vmem, name="residual", args=args, casts=casts[s])
        cast_out += cw
        h = outs[0]
        if emit_scaled:
            hg, ssq = outs[1], outs[2]
    return (h, hg, ssq), cast_out


def _up_kernel(hg_ref, ssq_ref, w_ref, o_ref):
    t = jnp.maximum(_dot(hg_ref[...], w_ref[...]) * _inv_rms(ssq_ref[...], hg_ref.shape[1]), 0.0)
    o_ref[...] = (t * t).astype(o_ref.dtype)


def _up(hg, ssq, w_up, *, casts=()):
    m, d = hg.shape
    f = w_up.shape[1]
    tm = _row_tile(m, ROW_TILE_UP)
    tn = min(f, COL_TILE_UP)
    vmem = 2 * tm * d * 2 + 2 * d * tn * 2 + 2 * tm * tn * 2 + 3 * tm * tn * 4 + 2 * MIB
    (hid,), cast_out = _call(
        _up_kernel, grid=(m // tm, f // tn),
        in_specs=[pl.BlockSpec((tm, d), lambda i, j: (i, 0)),
                  pl.BlockSpec((tm, 1), lambda i, j: (i, 0)),
                  pl.BlockSpec((d, tn), lambda i, j: (0, j))],
        out_specs=[pl.BlockSpec((tm, tn), lambda i, j: (i, j))],
        out_shape=[jax.ShapeDtypeStruct((m, f), BF16)],
        scratch=(), vmem=vmem, name="mlp_up", args=(hg, ssq, w_up), casts=casts)
    return hid, cast_out


def _rope_tables(pos, hd):
    inv_freq = ROPE_BASE ** (-jnp.arange(0, hd, 2, dtype=F32) / hd)
    ang = pos[:, None] * inv_freq[None, :]
    return jnp.cos(ang), jnp.sin(ang)


def kernel(x, meta_tokens, norm_mix_g, w_in, conv_w, w_ret_out, w_conv_out, w_out,
           norm_mlp_g, w_up, w_down, final_norm_g):
    batch, seq, d = x.shape
    depth = w_in.shape[0]
    n_meta = meta_tokens.shape[0]
    dr = w_ret_out.shape[1]
    dc = w_conv_out.shape[1]
    f = w_up.shape[2]
    hd = dr // RET_HEADS
    assert seq % CHUNK == 0 and n_meta <= CHUNK
    meta_pad = CHUNK - n_meta

    def cast(layer, name):
        spec = {
            "w_ret": (w_in, 0, 4, dr, hd, False),
            "w_cbz": (w_in, 4 * dr, 3, dc, _conv_tile(dc), False),
            "w_gate": (w_in, 4 * dr + 3 * dc, 2, d, _merge_tile(d), False),
            "w_ro": (w_ret_out, 0, 1, d, _merge_tile(d), False),
            "w_co": (w_conv_out, 0, 1, d, _merge_tile(d), False),
            "w_o": (w_out, 0, 1, d, min(d, COL_TILE_RES_O), True),
            "w_up": (w_up, 0, 1, f, min(f, COL_TILE_UP), False),
            "w_down": (w_down, 0, 1, d, min(d, COL_TILE_RES_DOWN), True),
        }[name]
        return _Cast(spec[0], layer, *spec[1:])

    lw = [dict() for _ in range(depth)]

    def riding(targets):
        targets = [(layer, name) for layer, name in targets if layer < depth]
        return targets, tuple(cast(layer, name) for layer, name in targets)

    def keep(targets, cast_out):
        for (layer, name), w in zip(targets, cast_out):
            lw[layer][name] = w

    log_gamma = jnp.asarray(
        np.log1p(-np.exp2(-5.0 - np.arange(RET_HEADS, dtype=np.float64))), F32)
    cos_x, sin_x = _rope_tables(jnp.arange(seq, dtype=F32) + n_meta, hd)
    cos_m, sin_m = _rope_tables(jnp.arange(CHUNK, dtype=F32) - meta_pad, hd)

    hm = jnp.concatenate([jnp.zeros((meta_pad, d), F32), meta_tokens.astype(F32)], axis=0)
    meta = dict(h=hm, batch=1, cos=cos_m, sin=sin_m, state=jnp.zeros((RET_HEADS, hd, hd), F32),
                halo=jnp.zeros((SUBLANES, dc), F32), is_meta=True)
    real = dict(h=x.reshape(batch * seq, d), batch=batch, cos=cos_x, sin=sin_x, is_meta=False)
    for st in (meta, real):
        targets, casts = riding([] if st["is_meta"] else [(0, "w_ret")])
        (st["hg"], st["ssq"]), out = _prep(st["h"], norm_mix_g[0], casts=casts)
        keep(targets, out)

    def stage(streams, fn, *target_groups):
        groups = [riding(t) for t in target_groups]
        for st in streams:
            casts = [() if st["is_meta"] else c for _, c in groups]
            updates, out = fn(st, *casts)
            st.update(updates)
            if not st["is_meta"]:
                keep([t for targets, _ in groups for t in targets], out)

    for i in range(depth):
        w = lw[i]
        nxt = i + 1
        streams = (meta, real)

        def ret(st, casts):
            outs, out = _retention(st["hg"], st["ssq"], w["w_ret"], st["cos"], st["sin"],
                                   st["state"], log_gamma, batch=st["batch"],
                                   emit_state=st["is_meta"], casts=casts)
            if st["is_meta"]:
                real["state"] = outs[1]
            return dict(a1=outs[0]), out

        def conv(st, casts):
            outs, out = _conv_mixer(st["hg"], st["ssq"], w["w_cbz"], conv_w[i], st["halo"],
                                    batch=st["batch"], emit_tail=st["is_meta"], casts=casts)
            if st["is_meta"]:
                real["halo"] = outs[1]
            return dict(b1=outs[0]), out

        def merge(st, casts):
            merged, out = _merge(st["a1"], st["b1"], st["hg"], st["ssq"], w["w_ro"], w["w_co"],
                                 w["w_gate"], casts=casts)
            return dict(merged=merged), out

        def project_out(st, casts):
            (h, hg, ssq), out = _residual(st["merged"], w["w_o"], st["h"], norm_mlp_g[i],
                                          k_splits=1, casts=(casts,))
            return dict(h=h, hg=hg, ssq=ssq), out

        def up(st, casts):
            hid, out = _up(st["hg"], st["ssq"], w["w_up"], casts=casts)
            return dict(hid=hid), out

        def down(st, *casts):
            g_next = norm_mix_g[nxt] if nxt < depth else None
            (h, hg, ssq), out = _residual(st["hid"], w["w_down"], st["h"], g_next,
                                          k_splits=K_SPLITS_DOWN, casts=casts)
            return dict(h=h, hg=hg, ssq=ssq), out

        def own(*names):
            return [(i, n) for n in names]

        stage(streams, ret, own("w_cbz", "w_up"))
        stage(streams, conv, own("w_gate", "w_ro", "w_co", "w_o", "w_down") + [(nxt, "w_ret")])
        if nxt == depth:
            streams = (real,)
        stage(streams, merge, [])
        stage(streams, project_out, [])
        stage(streams, up, [])
        stage(streams, down, *[[] for _ in range(K_SPLITS_DOWN)])
    return _rmsnorm(real["h"], final_norm_g).reshape(batch, seq, d)
```
